```python
import jax, jax.numpy as jnp
from jax import lax
import numpy as np

D_MODEL = 2048
BATCH = 8
SEQ = 4096
DEPTH = 2

CTX_LEN = 256
GRID_W = 64
EPS = 1e-6
WIDTH_A = D_MODEL // 2
HEAD_DIM_A = 128
N_HEADS_A = WIDTH_A // HEAD_DIM_A
SCAN_CHUNK = 64
WIDTH_B = D_MODEL // 2
CHUNK_B = 128
N_GROUPS_B = 8
GROUP_DIM_B = WIDTH_B // N_GROUPS_B
WIDTH_C = D_MODEL
CONV_W = 3
N_EXPERT_GROUPS = 4
EXPERTS_PER_GROUP = 8
N_EXPERTS = N_EXPERT_GROUPS * EXPERTS_PER_GROUP
TOP_K = 2
EXPERT_FF = 512
DISPATCH_BLOCK = 128
N_EVEN_LAYERS = (DEPTH + 1) // 2
N_ODD_LAYERS = DEPTH // 2

kernel_name = "hgrn2_gmlp_shortconv_hmoe_diffusion_trunk"


def _rms(x, g):
    xf = x.astype(jnp.float32)
    y = xf * lax.rsqrt(jnp.mean(xf * xf, axis=-1, keepdims=True) + EPS)
    return (y * g.astype(jnp.float32)).astype(x.dtype)


def _heads(t):
    b, L, _ = t.shape
    return t.reshape(b, L, N_HEADS_A, HEAD_DIM_A).transpose(0, 2, 1, 3)


def _flip(t):
    return None if t is None else t[:, :, ::-1]


def _forget(f_raw, lb):
    f = lb + (1.0 - lb) * jax.nn.sigmoid(f_raw)
    return _heads(1.0 - f), _heads(jnp.log(f))


def _gla_scan(q, k, v, logf, s0):
    b, h, L, dk = k.shape
    dv = v.shape[-1]
    n = L // SCAN_CHUNK

    def chunks(t):
        return t.reshape(b, h, n, SCAN_CHUNK, t.shape[-1])

    k, v, logf = chunks(k), chunks(v), chunks(logf)
    cum = jnp.cumsum(logf, axis=3)
    last = cum[:, :, :, -1:, :]
    d_state = jnp.einsum('bhnsk,bhnsv->nbhkv', k * jnp.exp(last - cum), v)
    decay = jnp.moveaxis(jnp.exp(last[:, :, :, 0, :]), 2, 0)
    collect = q is not None

    def step(s, inp):
        dec, ds = inp
        return dec[..., None] * s + ds, (s if collect else None)

    s_final, s_start = lax.scan(step, s0, (decay, d_state))
    if not collect:
        return None, s_final
    q_dec = chunks(q) * jnp.exp(cum)
    scores = jnp.einsum('bhntk,bhnsk->bhnts', q_dec, k * jnp.exp(-cum))
    tri = jnp.tril(jnp.ones((SCAN_CHUNK, SCAN_CHUNK), dtype=bool))
    o = jnp.einsum('bhnts,bhnsv->bhntv', jnp.where(tri, scores, 0.0), v)
    o = o + jnp.einsum('bhntk,nbhkv->bhntv', q_dec, s_start)
    return o.reshape(b, h, L, dv), s_final


def _hgrn_out(o, g, norm_g):
    b, _, L, _ = o.shape
    o = o.transpose(0, 2, 1, 3)
    o = o * lax.rsqrt(jnp.mean(o * o, axis=-1, keepdims=True) + EPS)
    o = o * norm_g.astype(jnp.float32).reshape(N_HEADS_A, HEAD_DIM_A)
    return o.reshape(b, L, WIDTH_A) * jax.nn.silu(g)


def _chunk_gmlp(p, ln_g, ln_b, w_s, b_s):
    b, L, _ = p.shape
    u, v = jnp.split(jax.nn.gelu(p.astype(jnp.float32)), 2, axis=-1)
    mu = jnp.mean(v, axis=-1, keepdims=True)
    var = jnp.mean(jnp.square(v - mu), axis=-1, keepdims=True)
    v = (v - mu) * lax.rsqrt(var + EPS) * ln_g.astype(jnp.float32) + ln_b.astype(jnp.float32)
    v = v.reshape(b, L // CHUNK_B, CHUNK_B, N_GROUPS_B, GROUP_DIM_B)
    s = jnp.einsum('gts,bnsgd->bntgd', w_s.astype(jnp.float32), v)
    s = s + b_s.astype(jnp.float32).T[:, :, None]
    return u * s.reshape(b, L, WIDTH_B)


def _even_mixer(h, h_ctx, w_in, w_out, lb, a_norm_g, ln_g, ln_b, w_s, b_s, ctx_out):
    f32 = jnp.float32
    na = 5 * WIDTH_A
    proj = h @ w_in
    q, f_fw, f_bw, v, g = jnp.split(proj[..., :na].astype(f32), 5, axis=-1)
    if ctx_out:
        proj_c = h_ctx @ w_in
        cq, cf_fw, cf_bw, cv, cg = jnp.split(proj_c[..., :na].astype(f32), 5, axis=-1)
        cq = _heads(jax.nn.silu(cq))
    else:
        cf_fw, cf_bw, cv = jnp.split((h_ctx @ w_in[:, WIDTH_A:4 * WIDTH_A]).astype(f32), 3, axis=-1)
        cq = None
    s0 = jnp.zeros((h.shape[0], N_HEADS_A, HEAD_DIM_A, HEAD_DIM_A), f32)
    k_cf, lf_cf = _forget(cf_fw, lb[0])
    k_cb, lf_cb = _forget(cf_bw, lb[1])
    cvh = _heads(cv)
    oc_f, st_f = _gla_scan(cq, k_cf, cvh, lf_cf, s0)
    oc_b, st_b = _gla_scan(_flip(cq), _flip(k_cb), _flip(cvh), _flip(lf_cb), s0)
    qh = _heads(jax.nn.silu(q))
    vh = _heads(v)
    k_f, lf_f = _forget(f_fw, lb[0])
    k_b, lf_b = _forget(f_bw, lb[1])
    o_f, _ = _gla_scan(qh, k_f, vh, lf_f, st_f)
    o_b, _ = _gla_scan(_flip(qh), _flip(k_b), _flip(vh), _flip(lf_b), st_b)
    a = _hgrn_out(o_f + _flip(o_b), g, a_norm_g)
    bm = _chunk_gmlp(proj[..., na:], ln_g, ln_b, w_s, b_s)
    y = jnp.concatenate([a, bm], axis=-1).astype(h.dtype) @ w_out
    y_ctx = None
    if ctx_out:
        ac = _hgrn_out(oc_f + _flip(oc_b), cg, a_norm_g)
        bc = _chunk_gmlp(proj_c[..., na:], ln_g, ln_b, w_s, b_s)
        y_ctx = jnp.concatenate([ac, bc], axis=-1).astype(h.dtype) @ w_out
    return y, y_ctx


def _short_conv(z, w, rows):
    b, L, ch = z.shape
    if rows is not None:
        z = z.reshape(b, rows, GRID_W, ch)
    n = z.shape[-2]
    half = CONV_W // 2
    pad = [(0, 0)] * (z.ndim - 2) + [(half, half), (0, 0)]
    zp = jnp.pad(z, pad)
    y = sum(zp[..., k:k + n, :] * w[k] for k in range(CONV_W))
    return y.reshape(b, L, ch)


def _gated_short_conv(h, w_in, conv_w, w_out, rows):
    hx, bg, cg = jnp.split(h @ w_in, 3, axis=-1)
    return (bg * _short_conv(cg * hx, conv_w, rows)) @ w_out


def _hier_moe(t, rg_w, rg_b, re_w, re_b, w_gate, w_up, w_down):
    n_tok, d = t.shape
    f32 = jnp.float32
    p_grp = jax.nn.softmax((t @ rg_w + rg_b).astype(f32), axis=-1)
    p_top, grp = lax.top_k(p_grp, 1)
    le = (t @ re_w + re_b).astype(f32).reshape(n_tok, N_EXPERT_GROUPS, EXPERTS_PER_GROUP)
    le = jnp.einsum('tge,tg->te', le, jax.nn.one_hot(grp[:, 0], N_EXPERT_GROUPS, dtype=f32))
    p_in, idx = lax.top_k(jax.nn.softmax(le, axis=-1), TOP_K)
    wts = (p_top * p_in / jnp.sum(p_in, axis=-1, keepdims=True)).reshape(-1)
    eid = (grp * EXPERTS_PER_GROUP + idx).reshape(-1)
    tok = jnp.repeat(jnp.arange(n_tok, dtype=jnp.int32), TOP_K)
    n_assign = n_tok * TOP_K
    order = jnp.argsort(eid)
    eid_s = eid[order]
    counts = jnp.bincount(eid, length=N_EXPERTS)
    starts = jnp.cumsum(counts) - counts
    padded = (counts + DISPATCH_BLOCK - 1) // DISPATCH_BLOCK * DISPATCH_BLOCK
    pend = jnp.cumsum(padded)
    pos = (pend - padded)[eid_s] + (jnp.arange(n_assign, dtype=jnp.int32) - starts[eid_s])
    n_blk = -(-n_assign // DISPATCH_BLOCK) + N_EXPERTS
    n_rows = n_blk * DISPATCH_BLOCK
    buf_tok = jnp.full((n_rows,), n_tok, jnp.int32).at[pos].set(tok[order])
    buf_w = jnp.zeros((n_rows,), f32).at[pos].set(wts[order])
    blk_e = jnp.searchsorted(pend, jnp.arange(n_blk, dtype=jnp.int32) * DISPATCH_BLOCK, side='right')
    blk_e = jnp.minimum(blk_e, N_EXPERTS - 1)
    t_pad = jnp.concatenate([t, jnp.zeros((1, d), t.dtype)], axis=0)
    xb = t_pad[buf_tok].reshape(n_blk, DISPATCH_BLOCK, d)

    def expert_block(args):
        xblk, e = args
        return (jax.nn.silu(xblk @ w_gate[e]) * (xblk @ w_up[e])) @ w_down[e]

    yb = lax.map(expert_block, (xb, blk_e)).reshape(n_rows, d)
    out = jnp.zeros((n_tok + 1, d), t.dtype).at[buf_tok].add(yb * buf_w[:, None].astype(t.dtype))
    return out[:n_tok]


def setup_inputs(seed: int = 0) -> dict:
    key = jax.random.key(seed)
    ks = jax.random.split(key, 27)
    D = D_MODEL
    ne, no = N_EVEN_LAYERS, N_ODD_LAYERS
    pe = 5 * WIDTH_A + 2 * WIDTH_B

    def n(k, shape, s):
        return jax.random.normal(k, shape, jnp.float32) * s

    return {
        "x": n(ks[0], (BATCH, SEQ, D), 1.0),
        "c": n(ks[1], (BATCH, D), 1.0),
        "ctx": n(ks[2], (BATCH, CTX_LEN, D), 1.0),
        "c_ctx": n(ks[3], (D,), 1.0),
        "w_mod": n(ks[4], (DEPTH, D, 6 * D), 0.5 * D ** -0.5),
        "b_mod": n(ks[5], (DEPTH, 6 * D), 0.02),
        "norm_mix_g": 1.0 + n(ks[6], (DEPTH, D), 0.02),
        "norm_ffn_g": 1.0 + n(ks[7], (DEPTH, D), 0.02),
        "final_g": 1.0 + n(ks[8], (D,), 0.02),
        "even_w_in": n(ks[9], (ne, D, pe), D ** -0.5),
        "even_w_out": n(ks[10], (ne, WIDTH_A + WIDTH_B, D), (WIDTH_A + WIDTH_B) ** -0.5),
        "hgrn_lb_raw": n(ks[11], (ne + 1, 2, WIDTH_A), 0.1),
        "hgrn_norm_g": 1.0 + n(ks[12], (ne, WIDTH_A), 0.02),
        "gmlp_ln_g": 1.0 + n(ks[13], (ne, WIDTH_B), 0.02),
        "gmlp_ln_b": n(ks[14], (ne, WIDTH_B), 0.02),
        "gmlp_w_s": n(ks[15], (ne, N_GROUPS_B, CHUNK_B, CHUNK_B), CHUNK_B ** -0.5),
        "gmlp_b_s": 1.0 + n(ks[16], (ne, N_GROUPS_B, CHUNK_B), 0.02),
        "odd_w_in": n(ks[17], (no, D, 3 * WIDTH_C), D ** -0.5),
        "odd_conv_w": n(ks[18], (no, CONV_W, WIDTH_C), CONV_W ** -0.5),
        "odd_w_out": n(ks[19], (no, WIDTH_C, D), WIDTH_C ** -0.5),
        "router_g_w": n(ks[20], (DEPTH, D, N_EXPERT_GROUPS), D ** -0.5),
        "router_g_b": n(ks[21], (DEPTH, N_EXPERT_GROUPS), 0.01),
        "router_e_w": n(ks[22], (DEPTH, D, N_EXPERTS), D ** -0.5),
        "router_e_b": n(ks[23], (DEPTH, N_EXPERTS), 0.01),
        "exp_w_gate": n(ks[24], (DEPTH, N_EXPERTS, D, EXPERT_FF), D ** -0.5),
        "exp_w_up": n(ks[25], (DEPTH, N_EXPERTS, D, EXPERT_FF), D ** -0.5),
        "exp_w_down": n(ks[26], (DEPTH, N_EXPERTS, EXPERT_FF, D), EXPERT_FF ** -0.5),
    }


def reference(x, c, ctx, c_ctx, w_mod, b_mod, norm_mix_g, norm_ffn_g, final_g,
              even_w_in, even_w_out, hgrn_lb_raw, hgrn_norm_g, gmlp_ln_g, gmlp_ln_b, gmlp_w_s, gmlp_b_s,
              odd_w_in, odd_conv_w, odd_w_out,
              router_g_w, router_g_b, router_e_w, router_e_b, exp_w_gate, exp_w_up, exp_w_down):
    b, seq, d = x.shape
    rows = seq // GRID_W
    n_ctx = ctx.shape[1]
    last_ctx_layer = ((DEPTH - 1) // 2) * 2
    lb_all = jnp.cumsum(jax.nn.softmax(hgrn_lb_raw.astype(jnp.float32), axis=0), axis=0)
    silu_c = jax.nn.silu(c)
    silu_cc = jax.nn.silu(c_ctx)
    hc = ctx
    for l in range(DEPTH):
        j = l // 2
        ctx_full = l < last_ctx_layer
        mod = silu_c @ w_mod[l] + b_mod[l]
        sh_m, sc_m, gt_m, sh_f, sc_f, gt_f = jnp.split(mod[:, None, :], 6, axis=-1)
        if ctx_full:
            mod_c = silu_cc @ w_mod[l] + b_mod[l]
            csh_m, csc_m, cgt_m, csh_f, csc_f, cgt_f = jnp.split(mod_c, 6)
        elif l == last_ctx_layer:
            csh_m, csc_m = jnp.split(silu_cc @ w_mod[l][:, :2 * d] + b_mod[l][:2 * d], 2)
        h = _rms(x, norm_mix_g[l]) * (1.0 + sc_m) + sh_m
        if l % 2 == 0:
            h_ctx = _rms(hc, norm_mix_g[l]) * (1.0 + csc_m) + csh_m
            y, y_ctx = _even_mixer(h, h_ctx, even_w_in[j], even_w_out[j], lb_all[j], hgrn_norm_g[j],
                                   gmlp_ln_g[j], gmlp_ln_b[j], gmlp_w_s[j], gmlp_b_s[j], ctx_full)
        else:
            y = _gated_short_conv(h, odd_w_in[j], odd_conv_w[j], odd_w_out[j], rows)
            if ctx_full:
                h_ctx = _rms(hc, norm_mix_g[l]) * (1.0 + csc_m) + csh_m
                y_ctx = _gated_short_conv(h_ctx, odd_w_in[j], odd_conv_w[j], odd_w_out[j], None)
        x = x + gt_m * y
        if ctx_full:
            hc = hc + cgt_m * y_ctx
        h = _rms(x, norm_ffn_g[l]) * (1.0 + sc_f) + sh_f
        tokens = h.reshape(b * seq, d)
        if ctx_full:
            hcf = _rms(hc, norm_ffn_g[l]) * (1.0 + csc_f) + csh_f
            tokens = jnp.concatenate([tokens, hcf.reshape(b * n_ctx, d)], axis=0)
        f = _hier_moe(tokens, router_g_w[l], router_g_b[l], router_e_w[l], router_e_b[l],
                      exp_w_gate[l], exp_w_up[l], exp_w_down[l])
        x = x + gt_f * f[:b * seq].reshape(b, seq, d)
        if ctx_full:
            hc = hc + cgt_f * f[b * seq:].reshape(b, n_ctx, d)
    return _rms(x, final_g)
```

```python
import functools

import jax
import jax.numpy as jnp
from jax import lax
from jax.experimental import pallas as pl
from jax.experimental.pallas import tpu as pltpu

f32 = jnp.float32
bf16 = jnp.bfloat16
i32 = jnp.int32

EPS = 1e-6
HEAD_DIM = 128
SCAN_CHUNK = 64
CHUNK_B = 128
N_GROUPS_B = 8
GRID_W = 64
N_EXPERT_GROUPS = 4
EXPERTS_PER_GROUP = 8
N_EXPERTS = N_EXPERT_GROUPS * EXPERTS_PER_GROUP
TOP_K = 2
ROUTE_LANES = 128
MOE_BLOCK = 256
VMEM_LIMIT = 56 * 1024 * 1024

NT = (((1,), (1,)), ((), ()))
TN = (((0,), (0,)), ((), ()))


def _cparams(n_axes):
    return pltpu.CompilerParams(dimension_semantics=("arbitrary",) * n_axes, vmem_limit_bytes=VMEM_LIMIT)


def _silu(v):
    return v * jax.nn.sigmoid(v)


def _mod_kernel(c_ref, w_ref, b_ref, o_ref):
    s = _silu(c_ref[...])
    o_ref[0] = jnp.dot(s, w_ref[0], preferred_element_type=f32, precision=lax.Precision.HIGHEST) + b_ref[0]


def _mod(c_all, w_mod, b_mod):
    depth, d, n = w_mod.shape
    rows = c_all.shape[0]
    tn = 1024
    return pl.pallas_call(
        _mod_kernel,
        grid=(depth, n // tn),
        in_specs=[
            pl.BlockSpec((rows, d), lambda l, j: (0, 0)),
            pl.BlockSpec((1, d, tn), lambda l, j: (l, 0, j)),
            pl.BlockSpec((1, 1, tn), lambda l, j: (l, 0, j)),
        ],
        out_specs=pl.BlockSpec((1, rows, tn), lambda l, j: (l, 0, j)),
        out_shape=jax.ShapeDtypeStruct((depth, rows, n), f32),
        compiler_params=_cparams(2),
        name="mod",
    )(c_all, w_mod, b_mod.reshape(depth, 1, n))


def _rms_mod(x, g, sc, sh):
    y = x * lax.rsqrt(jnp.mean(x * x, axis=-1, keepdims=True) + EPS)
    return (y * g) * (1.0 + sc) + sh


def _norm_matmul_kernel(x_ref, g_ref, sc_ref, sh_ref, w_ref, o_ref, h_ref):
    @pl.when(pl.program_id(2) == 0)
    def _():
        h_ref[...] = _rms_mod(x_ref[0], g_ref[...], sc_ref[0], sh_ref[0]).astype(bf16)

    o_ref[0] = jnp.dot(h_ref[...], w_ref[...], preferred_element_type=f32).astype(o_ref.dtype)


def _norm_matmul(x, g, sc, sh, w, tm, tn):
    b, l, d = x.shape
    n = w.shape[1]
    return pl.pallas_call(
        _norm_matmul_kernel,
        grid=(b, l // tm, n // tn),
        in_specs=[
            pl.BlockSpec((1, tm, d), lambda bi, i, j: (bi, i, 0)),
            pl.BlockSpec((1, d), lambda bi, i, j: (0, 0)),
            pl.BlockSpec((1, 1, d), lambda bi, i, j: (bi, 0, 0)),
            pl.BlockSpec((1, 1, d), lambda bi, i, j: (bi, 0, 0)),
            pl.BlockSpec((d, tn), lambda bi, i, j: (0, j)),
        ],
        out_specs=pl.BlockSpec((1, tm, tn), lambda bi, i, j: (bi, i, j)),
        out_shape=jax.ShapeDtypeStruct((b, l, n), bf16),
        scratch_shapes=[pltpu.VMEM((tm, d), bf16)],
        compiler_params=_cparams(3),
        name="norm_matmul",
    )(x, g.reshape(1, d), sc, sh, w)


def _hgrn_kernel(q_ref, ff_ref, fb_ref, v_ref, g_ref, cff_ref, cfb_ref, cv_ref, lb_ref, ng_ref,
                 a_ref, o_acc, st_ref):
    C = SCAN_CHUNK
    L = q_ref.shape[1]
    Lc = cff_ref.shape[1]
    n, nc = L // C, Lc // C
    row = lax.broadcasted_iota(i32, (C, C), 0)
    col = lax.broadcasted_iota(i32, (C, C), 1)
    lower = col <= row
    upper = col >= row
    lower_f = lower.astype(f32)
    upper_f = upper.astype(f32)
    lb_f = lb_ref[0:1, :]
    lb_b = lb_ref[1:2, :]

    def gates(f_raw, lb, cmat):
        f = lb + (1.0 - lb) * jax.nn.sigmoid(f_raw.astype(f32))
        cum = jnp.dot(cmat, jnp.log(f), preferred_element_type=f32, precision=lax.Precision.HIGHEST)
        return 1.0 - f, cum

    def new_state(st, k, cum, edge, v):
        k_st = (k * jnp.exp(edge - cum)).astype(bf16)
        return jnp.exp(edge) * st + lax.dot_general(v, k_st, TN, preferred_element_type=f32)

    def chunk_out(q, k, cum, v, st, mask):
        q_dec = (q * jnp.exp(cum)).astype(bf16)
        k_dec = (k * jnp.exp(-cum)).astype(bf16)
        s = lax.dot_general(q_dec, k_dec, NT, preferred_element_type=f32)
        s = jnp.where(mask, s, 0.0).astype(bf16)
        return (jnp.dot(s, v, preferred_element_type=f32)
                + lax.dot_general(q_dec, st.astype(bf16), NT, preferred_element_type=f32))

    st_f = jnp.zeros((HEAD_DIM, HEAD_DIM), f32)
    st_b = jnp.zeros((HEAD_DIM, HEAD_DIM), f32)
    for c in range(nc):
        sl = slice(c * C, (c + 1) * C)
        k, cum = gates(cff_ref[0, sl, :], lb_f, lower_f)
        st_f = new_state(st_f, k, cum, cum[C - 1:C, :], cv_ref[0, sl, :])
        sl = slice((nc - 1 - c) * C, (nc - c) * C)
        k, cum = gates(cfb_ref[0, sl, :], lb_b, upper_f)
        st_b = new_state(st_b, k, cum, cum[0:1, :], cv_ref[0, sl, :])
    st_ref[0] = st_f
    st_ref[1] = st_b
    o_acc[...] = jnp.zeros_like(o_acc)

    def body(c, carry):
        off = pl.multiple_of(c * C, C)
        v = v_ref[0, pl.ds(off, C), :]
        q = _silu(q_ref[0, pl.ds(off, C), :].astype(f32))
        k, cum = gates(ff_ref[0, pl.ds(off, C), :], lb_f, lower_f)
        st = st_ref[0]
        o_acc[pl.ds(off, C), :] += chunk_out(q, k, cum, v, st, lower)
        st_ref[0] = new_state(st, k, cum, cum[C - 1:C, :], v)
        off = pl.multiple_of((n - 1 - c) * C, C)
        v = v_ref[0, pl.ds(off, C), :]
        q = _silu(q_ref[0, pl.ds(off, C), :].astype(f32))
        k, cum = gates(fb_ref[0, pl.ds(off, C), :], lb_b, upper_f)
        st = st_ref[1]
        o_acc[pl.ds(off, C), :] += chunk_out(q, k, cum, v, st, upper)
        st_ref[1] = new_state(st, k, cum, cum[0:1, :], v)
        return carry

    lax.fori_loop(0, n, body, 0)

    R = 512 if L % 512 == 0 else C

    def epilogue(r, carry):
        off = pl.multiple_of(r * R, R)
        o = o_acc[pl.ds(off, R), :]
        o = o * lax.rsqrt(jnp.mean(o * o, axis=-1, keepdims=True) + EPS)
        o = o * ng_ref[...]
        a_ref[0, pl.ds(off, R), :] = (o * _silu(g_ref[0, pl.ds(off, R), :].astype(f32))).astype(a_ref.dtype)
        return carry

    lax.fori_loop(0, L // R, epilogue, 0)


def _hgrn(proj, proj_c, lb, norm_g, width_a):
    b, l, _ = proj.shape
    lc = proj_c.shape[1]
    nh = width_a // HEAD_DIM

    def seg(k):
        return pl.BlockSpec((1, l, HEAD_DIM), lambda bi, h, k=k: (bi, 0, k * nh + h))

    def cseg(k):
        return pl.BlockSpec((1, lc, HEAD_DIM), lambda bi, h, k=k: (bi, 0, k * nh + h))

    return pl.pallas_call(
        _hgrn_kernel,
        grid=(b, nh),
        in_specs=[seg(0), seg(1), seg(2), seg(3), seg(4), cseg(0), cseg(1), cseg(2),
                  pl.BlockSpec((2, HEAD_DIM), lambda bi, h: (0, h)),
                  pl.BlockSpec((1, HEAD_DIM), lambda bi, h: (0, h))],
        out_specs=pl.BlockSpec((1, l, HEAD_DIM), lambda bi, h: (bi, 0, h)),
        out_shape=jax.ShapeDtypeStruct((b, l, width_a), bf16),
        scratch_shapes=[pltpu.VMEM((l, HEAD_DIM), f32), pltpu.VMEM((2, HEAD_DIM, HEAD_DIM), f32)],
        compiler_params=_cparams(2),
        name="hgrn",
    )(proj, proj, proj, proj, proj, proj_c, proj_c, proj_c, lb, norm_g.reshape(1, width_a))


def _residual_norm_route(x, y, gt, g2, sc, sh, wr_ref, br_ref, x1_ref, h2_ref, route_ref):
    x1 = x + gt * y
    x1_ref[0] = x1
    h2 = _rms_mod(x1, g2, sc, sh)
    h2_ref[0] = h2
    logits = jnp.dot(h2.astype(bf16), wr_ref[...], preferred_element_type=f32) + br_ref[...]
    tm = logits.shape[0]
    lane = lax.broadcasted_iota(i32, (tm, ROUTE_LANES), 1)
    lane_f = lane.astype(f32)
    neg = -jnp.inf
    gl = jnp.where(lane < N_EXPERT_GROUPS, logits, neg)
    gmax = jnp.max(gl, axis=-1, keepdims=True)
    p_top = 1.0 / jnp.sum(jnp.exp(gl - gmax), axis=-1, keepdims=True)
    grp = jnp.min(jnp.where(gl == gmax, lane_f, float(ROUTE_LANES)), axis=-1, keepdims=True)
    lo = float(N_EXPERT_GROUPS) + grp * float(EXPERTS_PER_GROUP)
    emask = (lane_f >= lo) & (lane_f < lo + float(EXPERTS_PER_GROUP))
    el = jnp.where(emask, logits, neg)
    emax = jnp.max(el, axis=-1, keepdims=True)
    esum = jnp.sum(jnp.exp(el - emax), axis=-1, keepdims=True)
    i1 = jnp.min(jnp.where(el == emax, lane_f, float(ROUTE_LANES)), axis=-1, keepdims=True)
    p1 = 1.0 / esum
    el2 = jnp.where(lane_f == i1, neg, el)
    emax2 = jnp.max(el2, axis=-1, keepdims=True)
    i2 = jnp.min(jnp.where(el2 == emax2, lane_f, float(ROUTE_LANES)), axis=-1, keepdims=True)
    p2 = jnp.exp(emax2 - emax) / esum
    den = p1 + p2
    w1 = p_top * p1 / den
    w2 = p_top * p2 / den
    e1 = i1 - float(N_EXPERT_GROUPS)
    e2 = i2 - float(N_EXPERT_GROUPS)
    route = jnp.where(lane == 0, e1, jnp.where(lane == 1, e2, jnp.where(lane == 2, w1, jnp.where(lane == 3, w2, 0.0))))
    route_ref[0] = route


def _gelu_tanh(v):
    return 0.5 * v * (1.0 + jnp.tanh(0.7978845608028654 * (v + 0.044715 * (v * v * v))))


def _even_out_kernel(x_ref, a_ref, u_ref, v_ref, lng_ref, lnb_ref, ws_ref, bs_ref, wo_ref, gt_ref,
                     g2_ref, sc_ref, sh_ref, wr_ref, br_ref, x1_ref, h2_ref, route_ref, bm_ref):
    tm = x_ref.shape[1]
    wb = u_ref.shape[2]
    gd = wb // N_GROUPS_B
    gv = _gelu_tanh(v_ref[0].astype(f32))
    mu = jnp.mean(gv, axis=-1, keepdims=True)
    dv = gv - mu
    var = jnp.mean(dv * dv, axis=-1, keepdims=True)
    vn = (dv * lax.rsqrt(var + EPS) * lng_ref[...] + lnb_ref[...]).astype(bf16)
    for nb in range(tm // CHUNK_B):
        rs = slice(nb * CHUNK_B, (nb + 1) * CHUNK_B)
        for g in range(N_GROUPS_B):
            cs = slice(g * gd, (g + 1) * gd)
            s = jnp.dot(ws_ref[g], vn[rs, cs], preferred_element_type=f32) + bs_ref[:, g:g + 1]
            bm_ref[rs, cs] = (_gelu_tanh(u_ref[0, rs, cs].astype(f32)) * s).astype(bf16)
    wa = a_ref.shape[2]
    y = (jnp.dot(a_ref[0], wo_ref[0:wa, :], preferred_element_type=f32)
         + jnp.dot(bm_ref[...], wo_ref[wa:, :], preferred_element_type=f32))
    _residual_norm_route(x_ref[0], y, gt_ref[0], g2_ref[...], sc_ref[0], sh_ref[0], wr_ref, br_ref,
                         x1_ref, h2_ref, route_ref)


def _tail_specs(d, tm):
    vec = pl.BlockSpec((1, d), lambda bi, i: (0, 0))
    per_b = pl.BlockSpec((1, 1, d), lambda bi, i: (bi, 0, 0))
    in_specs = [per_b, vec, per_b, per_b,
                pl.BlockSpec((d, ROUTE_LANES), lambda bi, i: (0, 0)),
                pl.BlockSpec((1, ROUTE_LANES), lambda bi, i: (0, 0))]
    out_specs = [pl.BlockSpec((1, tm, d), lambda bi, i: (bi, i, 0)),
                 pl.BlockSpec((1, tm, d), lambda bi, i: (bi, i, 0)),
                 pl.BlockSpec((1, tm, ROUTE_LANES), lambda bi, i: (bi, i, 0))]
    return in_specs, out_specs


def _tail_out_shapes(b, l, d):
    return [jax.ShapeDtypeStruct((b, l, d), f32), jax.ShapeDtypeStruct((b, l, d), f32),
            jax.ShapeDtypeStruct((b, l, ROUTE_LANES), f32)]


def _even_out(x, a, proj, ln_g, ln_b, w_s, b_s_t, w_out, gt, g2, sc, sh, wr, br, tm):
    b, l, d = x.shape
    wa = a.shape[2]
    wb = ln_g.shape[0]
    ub = (5 * wa) // wb
    tail_in, tail_out = _tail_specs(d, tm)
    return pl.pallas_call(
        _even_out_kernel,
        grid=(b, l // tm),
        in_specs=[
            pl.BlockSpec((1, tm, d), lambda bi, i: (bi, i, 0)),
            pl.BlockSpec((1, tm, wa), lambda bi, i: (bi, i, 0)),
            pl.BlockSpec((1, tm, wb), lambda bi, i: (bi, i, ub)),
            pl.BlockSpec((1, tm, wb), lambda bi, i: (bi, i, ub + 1)),
            pl.BlockSpec((1, wb), lambda bi, i: (0, 0)),
            pl.BlockSpec((1, wb), lambda bi, i: (0, 0)),
            pl.BlockSpec((N_GROUPS_B, CHUNK_B, CHUNK_B), lambda bi, i: (0, 0, 0)),
            pl.BlockSpec((CHUNK_B, N_GROUPS_B), lambda bi, i: (0, 0)),
            pl.BlockSpec((wa + wb, d), lambda bi, i: (0, 0)),
        ] + tail_in,
        out_specs=tail_out,
        out_shape=_tail_out_shapes(b, l, d),
        scratch_shapes=[pltpu.VMEM((tm, wb), bf16)],
        compiler_params=_cparams(2),
        name="even_out",
    )(x, a, proj, proj, ln_g.reshape(1, wb), ln_b.reshape(1, wb), w_s, b_s_t, w_out, gt, g2.reshape(1, d), sc, sh, wr, br)


def _odd_out_kernel(hx_ref, bg_ref, cg_ref, cw_ref, x_ref, wo_ref, gt_ref,
                    g2_ref, sc_ref, sh_ref, wr_ref, br_ref, x1_ref, h2_ref, route_ref):
    tm = x_ref.shape[1]
    z = cg_ref[0].astype(f32) * hx_ref[0].astype(f32)
    pos = lax.broadcasted_iota(i32, (tm, 1), 0) % GRID_W
    z_prev = jnp.where(pos == 0, 0.0, pltpu.roll(z, 1, 0))
    z_next = jnp.where(pos == GRID_W - 1, 0.0, pltpu.roll(z, tm - 1, 0))
    conv = z_prev * cw_ref[0:1, :] + z * cw_ref[1:2, :] + z_next * cw_ref[2:3, :]
    t = (bg_ref[0].astype(f32) * conv).astype(bf16)
    y = jnp.dot(t, wo_ref[...], preferred_element_type=f32)
    _residual_norm_route(x_ref[0], y, gt_ref[0], g2_ref[...], sc_ref[0], sh_ref[0], wr_ref, br_ref,
                         x1_ref, h2_ref, route_ref)


def _odd_out(x, pc, conv_w, w_out, gt, g2, sc, sh, wr, br, tm):
    b, l, d = x.shape
    wc = w_out.shape[0]
    tail_in, tail_out = _tail_specs(d, tm)
    return pl.pallas_call(
        _odd_out_kernel,
        grid=(b, l // tm),
        in_specs=[
            pl.BlockSpec((1, tm, wc), lambda bi, i: (bi, i, 0)),
            pl.BlockSpec((1, tm, wc), lambda bi, i: (bi, i, 1)),
            pl.BlockSpec((1, tm, wc), lambda bi, i: (bi, i, 2)),
            pl.BlockSpec((conv_w.shape[0], wc), lambda bi, i: (0, 0)),
            pl.BlockSpec((1, tm, d), lambda bi, i: (bi, i, 0)),
            pl.BlockSpec((wc, d), lambda bi, i: (0, 0)),
        ] + tail_in,
        out_specs=tail_out,
        out_shape=_tail_out_shapes(b, l, d),
        compiler_params=_cparams(2),
        name="odd_out",
    )(pc, pc, pc, conv_w, x, w_out, gt, g2.reshape(1, d), sc, sh, wr, br)


def _row_gather_start(idx_ref, n_rows, src_hbm, dst_buf, sem):
    def body(r, carry):
        t = idx_ref[0, 0, r]
        pltpu.make_async_copy(src_hbm.at[pl.ds(t, 1), :], dst_buf.at[pl.ds(r, 1), :], sem).start()
        return carry

    lax.fori_loop(0, n_rows, body, 0, unroll=8)


def _row_gather_wait(n_rows, src_hbm, dst_buf, sem):
    def body(r, carry):
        pltpu.make_async_copy(src_hbm.at[pl.ds(0, 1), :], dst_buf.at[pl.ds(r, 1), :], sem).wait()
        return carry

    lax.fori_loop(0, n_rows, body, 0, unroll=8)


def _ffn_kernel(blk_e_ref, nused_ref, tok_cur_ref, tok_nxt_ref, h_hbm, wg_ref, wu_ref, wd_ref, rw_ref,
                y_ref, xbuf, sem, wg_s, wu_s, wd_s):
    i = pl.program_id(0)
    nu = nused_ref[0]
    bm = y_ref.shape[0]
    slot = lax.rem(i, 2)

    @pl.when(i == 0)
    def _():
        _row_gather_start(tok_cur_ref, bm, h_hbm, xbuf.at[0], sem.at[0])

    @pl.when(i + 1 < nu)
    def _():
        _row_gather_start(tok_nxt_ref, bm, h_hbm, xbuf.at[1 - slot], sem.at[1 - slot])

    @pl.when(i < nu)
    def _():
        e = blk_e_ref[i]
        e_prev = blk_e_ref[jnp.maximum(i - 1, 0)]

        @pl.when((i == 0) | (e != e_prev))
        def _():
            wg_s[...] = wg_ref[0].astype(bf16)
            wu_s[...] = wu_ref[0].astype(bf16)
            wd_s[...] = wd_ref[0].astype(bf16)

        _row_gather_wait(bm, h_hbm, xbuf.at[slot], sem.at[slot])
        xb = xbuf[slot].astype(bf16)
        gate = jnp.dot(xb, wg_s[...], preferred_element_type=f32)
        up = jnp.dot(xb, wu_s[...], preferred_element_type=f32)
        mid = (_silu(gate) * up).astype(bf16)
        y_ref[...] = jnp.dot(mid, wd_s[...], preferred_element_type=f32) * rw_ref[...]

    @pl.when(i >= nu)
    def _():
        y_ref[...] = jnp.zeros_like(y_ref)


def _moe_ffn(h2, row_tok, row_w, blk_e, n_used, w_gate, w_up, w_down):
    t, d = h2.shape
    n_blk = blk_e.shape[0]
    bm = MOE_BLOCK
    ff = w_gate.shape[2]
    tok3 = row_tok.reshape(n_blk, 1, bm)
    grid_spec = pltpu.PrefetchScalarGridSpec(
        num_scalar_prefetch=2,
        grid=(n_blk,),
        in_specs=[
            pl.BlockSpec((1, 1, bm), lambda i, be, nu: (i, 0, 0), memory_space=pltpu.SMEM),
            pl.BlockSpec((1, 1, bm), lambda i, be, nu: (jnp.minimum(i + 1, n_blk - 1), 0, 0), memory_space=pltpu.SMEM),
            pl.BlockSpec(memory_space=pl.ANY),
            pl.BlockSpec((1, d, ff), lambda i, be, nu: (be[i], 0, 0)),
            pl.BlockSpec((1, d, ff), lambda i, be, nu: (be[i], 0, 0)),
            pl.BlockSpec((1, ff, d), lambda i, be, nu: (be[i], 0, 0)),
            pl.BlockSpec((bm, 1), lambda i, be, nu: (i, 0)),
        ],
        out_specs=pl.BlockSpec((bm, d), lambda i, be, nu: (i, 0)),
        scratch_shapes=[pltpu.VMEM((2, bm, d), f32), pltpu.SemaphoreType.DMA((2,)),
                        pltpu.VMEM((d, ff), bf16), pltpu.VMEM((d, ff), bf16), pltpu.VMEM((ff, d), bf16)],
    )
    return pl.pallas_call(
        _ffn_kernel,
        grid_spec=grid_spec,
        out_shape=jax.ShapeDtypeStruct((n_blk * bm, d), f32),
        compiler_params=_cparams(1),
        name="moe_ffn",
    )(blk_e, n_used, tok3, tok3, h2, w_gate, w_up, w_down, row_w.reshape(n_blk * bm, 1))


def _combine_kernel(pos_cur_ref, pos_nxt_ref, x_ref, gt_ref, fg_ref, ys_hbm, o_ref, ybuf, sem, *, final):
    i = pl.program_id(0)
    n = pl.num_programs(0)
    tm = x_ref.shape[1]
    slot = lax.rem(i, 2)

    @pl.when(i == 0)
    def _():
        _row_gather_start(pos_cur_ref, TOP_K * tm, ys_hbm, ybuf.at[0], sem.at[0])

    @pl.when(i + 1 < n)
    def _():
        _row_gather_start(pos_nxt_ref, TOP_K * tm, ys_hbm, ybuf.at[1 - slot], sem.at[1 - slot])

    _row_gather_wait(TOP_K * tm, ys_hbm, ybuf.at[slot], sem.at[slot])
    f = ybuf[slot, 0:tm, :] + ybuf[slot, tm:2 * tm, :]
    xo = x_ref[0] + gt_ref[0] * f
    if final:
        xo = (xo * lax.rsqrt(jnp.mean(xo * xo, axis=-1, keepdims=True) + EPS)) * fg_ref[...]
    o_ref[0] = xo


def _moe_combine(x1, gt, final_g, ys, pos, tm, final):
    b, l, d = x1.shape
    nt = (b * l) // tm
    per_b = l // tm
    pos3 = pos.reshape(nt, tm, TOP_K).transpose(0, 2, 1).reshape(nt, 1, TOP_K * tm)
    return pl.pallas_call(
        functools.partial(_combine_kernel, final=final),
        grid=(nt,),
        in_specs=[
            pl.BlockSpec((1, 1, TOP_K * tm), lambda i: (i, 0, 0), memory_space=pltpu.SMEM),
            pl.BlockSpec((1, 1, TOP_K * tm), lambda i: (jnp.minimum(i + 1, nt - 1), 0, 0), memory_space=pltpu.SMEM),
            pl.BlockSpec((1, tm, d), lambda i: (i // per_b, i % per_b, 0)),
            pl.BlockSpec((1, 1, d), lambda i: (i // per_b, 0, 0)),
            pl.BlockSpec((1, d), lambda i: (0, 0)),
            pl.BlockSpec(memory_space=pl.ANY),
        ],
        out_specs=pl.BlockSpec((1, tm, d), lambda i: (i // per_b, i % per_b, 0)),
        out_shape=jax.ShapeDtypeStruct((b, l, d), f32),
        scratch_shapes=[pltpu.VMEM((2, TOP_K * tm, d), f32), pltpu.SemaphoreType.DMA((2,))],
        compiler_params=_cparams(1),
        name="moe_combine",
    )(pos3, pos3, x1, gt, final_g.reshape(1, d), ys)


def _dispatch_plan(route, n_tok):
    bm = MOE_BLOCK
    eid = route[..., 0:TOP_K].astype(i32).reshape(n_tok * TOP_K)
    wts = route[..., TOP_K:2 * TOP_K].reshape(n_tok * TOP_K)
    n_assign = n_tok * TOP_K
    order = jnp.argsort(eid, stable=True).astype(i32)
    eid_s = eid[order]
    counts = jnp.bincount(eid, length=N_EXPERTS).astype(i32)
    starts = jnp.cumsum(counts) - counts
    padded = (counts + bm - 1) // bm * bm
    pend = jnp.cumsum(padded)
    pos_s = (pend - padded)[eid_s] + (jnp.arange(n_assign, dtype=i32) - starts[eid_s])
    n_blk = -(-n_assign // bm) + N_EXPERTS
    n_rows = n_blk * bm
    row_tok = jnp.zeros((n_rows,), i32).at[pos_s].set(order // TOP_K)
    row_w = jnp.zeros((n_rows,), f32).at[pos_s].set(wts[order])
    pos = jnp.zeros((n_assign,), i32).at[order].set(pos_s).reshape(n_tok, TOP_K)
    blk_e = jnp.searchsorted(pend, jnp.arange(n_blk, dtype=i32) * bm, side="right").astype(i32)
    blk_e = jnp.minimum(blk_e, N_EXPERTS - 1)
    n_used = (pend[-1] // bm).astype(i32).reshape(1)
    return row_tok, row_w, pos, blk_e, n_used


def _hier_moe(x1, h2, route, gt, final_g, w_gate, w_up, w_down, final):
    b, l, d = x1.shape
    n_tok = b * l
    row_tok, row_w, pos, blk_e, n_used = _dispatch_plan(route, n_tok)
    ys = _moe_ffn(h2.reshape(n_tok, d), row_tok, row_w, blk_e, n_used, w_gate, w_up, w_down)
    return _moe_combine(x1, gt, final_g, ys, pos, 256, final)


def _router_params(rg_w, rg_b, re_w, re_b):
    d = rg_w.shape[0]
    pad = ROUTE_LANES - N_EXPERT_GROUPS - N_EXPERTS
    wr = jnp.concatenate([rg_w, re_w, jnp.zeros((d, pad), f32)], axis=1).astype(bf16)
    br = jnp.concatenate([rg_b, re_b, jnp.zeros((pad,), f32)]).reshape(1, ROUTE_LANES)
    return wr, br


def kernel(x, c, ctx, c_ctx, w_mod, b_mod, norm_mix_g, norm_ffn_g, final_g, even_w_in, even_w_out, hgrn_lb_raw,
           hgrn_norm_g, gmlp_ln_g, gmlp_ln_b, gmlp_w_s, gmlp_b_s, odd_w_in, odd_conv_w, odd_w_out, router_g_w,
           router_g_b, router_e_w, router_e_b, exp_w_gate, exp_w_up, exp_w_down):
    b, l, d = x.shape
    depth = w_mod.shape[0]
    assert depth == 2, "layer plan below is written for one even and one odd layer"
    width_a = hgrn_norm_g.shape[1]
    lc = ctx.shape[1]

    rows = -(-(b + 1) // 8) * 8
    c_all = jnp.zeros((rows, d), f32).at[:b].set(c).at[b].set(c_ctx)
    mod = _mod(c_all, w_mod, b_mod)

    def latent_mod(layer):
        return [m.reshape(b, 1, d) for m in jnp.split(mod[layer, :b], 6, axis=-1)]

    sh_m, sc_m, gt_m, sh_f, sc_f, gt_f = latent_mod(0)
    csh_m = jnp.broadcast_to(mod[0, b, 0:d].reshape(1, 1, d), (b, 1, d))
    csc_m = jnp.broadcast_to(mod[0, b, d:2 * d].reshape(1, 1, d), (b, 1, d))
    w_in = even_w_in[0].astype(bf16)
    proj = _norm_matmul(x, norm_mix_g[0], sc_m, sh_m, w_in, 512, 1024)
    proj_c = _norm_matmul(ctx, norm_mix_g[0], csc_m, csh_m, w_in[:, width_a:4 * width_a], lc, 1024)
    lb = jnp.cumsum(jax.nn.softmax(hgrn_lb_raw.astype(f32), axis=0), axis=0)[0]
    a = _hgrn(proj, proj_c, lb, hgrn_norm_g[0], width_a)
    wr, br = _router_params(router_g_w[0], router_g_b[0], router_e_w[0], router_e_b[0])
    x1, h2, route = _even_out(x, a, proj, gmlp_ln_g[0], gmlp_ln_b[0], gmlp_w_s[0].astype(bf16), gmlp_b_s[0].T,
                              even_w_out[0].astype(bf16), gt_m, norm_ffn_g[0], sc_f, sh_f, wr, br, 512)
    x = _hier_moe(x1, h2, route, gt_f, final_g, exp_w_gate[0], exp_w_up[0], exp_w_down[0], False)

    sh_m, sc_m, gt_m, sh_f, sc_f, gt_f = latent_mod(1)
    pc = _norm_matmul(x, norm_mix_g[1], sc_m, sh_m, odd_w_in[0].astype(bf16), 512, 1024)
    wr, br = _router_params(router_g_w[1], router_g_b[1], router_e_w[1], router_e_b[1])
    x1, h2, route = _odd_out(x, pc, odd_conv_w[0], odd_w_out[0].astype(bf16), gt_m, norm_ffn_g[1], sc_f, sh_f,
                             wr, br, 512)
    return _hier_moe(x1, h2, route, gt_f, final_g, exp_w_gate[1], exp_w_up[1], exp_w_down[1], True)
```

```python
import functools

import jax
import jax.numpy as jnp
from jax import lax
from jax.experimental import pallas as pl
from jax.experimental.pallas import tpu as pltpu

f32 = jnp.float32
bf16 = jnp.bfloat16
i32 = jnp.int32

EPS = 1e-6
HEAD_DIM = 128
SCAN_CHUNK = 64
GATE_ROWS = 128
CHUNK_B = 128
N_GROUPS_B = 8
GRID_W = 64
N_EXPERT_GROUPS = 4
EXPERTS_PER_GROUP = 8
N_EXPERTS = N_EXPERT_GROUPS * EXPERTS_PER_GROUP
TOP_K = 2
ROUTE_LANES = 128
MOE_BLOCK = 256
VMEM_LIMIT = 56 * 1024 * 1024

NT = (((1,), (1,)), ((), ()))
TN = (((0,), (0,)), ((), ()))


def _cparams(n_axes):
    return pltpu.CompilerParams(dimension_semantics=("arbitrary",) * n_axes, vmem_limit_bytes=VMEM_LIMIT)


def _silu(v):
    return v * jax.nn.sigmoid(v)


def _mod_kernel(c_ref, w_ref, b_ref, o_ref):
    s = _silu(c_ref[...])
    o_ref[0] = jnp.dot(s, w_ref[0], preferred_element_type=f32, precision=lax.Precision.HIGHEST) + b_ref[0]


def _mod(c_all, w_mod, b_mod):
    depth, d, n = w_mod.shape
    rows = c_all.shape[0]
    tn = 1024
    return pl.pallas_call(
        _mod_kernel,
        grid=(depth, n // tn),
        in_specs=[
            pl.BlockSpec((rows, d), lambda l, j: (0, 0)),
            pl.BlockSpec((1, d, tn), lambda l, j: (l, 0, j)),
            pl.BlockSpec((1, 1, tn), lambda l, j: (l, 0, j)),
        ],
        out_specs=pl.BlockSpec((1, rows, tn), lambda l, j: (l, 0, j)),
        out_shape=jax.ShapeDtypeStruct((depth, rows, n), f32),
        compiler_params=_cparams(2),
        name="mod",
    )(c_all, w_mod, b_mod.reshape(depth, 1, n))


def _rms_mod(x, g, sc, sh):
    y = x * lax.rsqrt(jnp.mean(x * x, axis=-1, keepdims=True) + EPS)
    return (y * g) * (1.0 + sc) + sh


def _norm_matmul_kernel(x_ref, g_ref, sc_ref, sh_ref, w_ref, o_ref, h_ref):
    @pl.when(pl.program_id(2) == 0)
    def _():
        h_ref[...] = _rms_mod(x_ref[0], g_ref[...], sc_ref[0], sh_ref[0]).astype(bf16)

    o_ref[0] = jnp.dot(h_ref[...], w_ref[...], preferred_element_type=f32).astype(o_ref.dtype)


def _norm_matmul(x, g, sc, sh, w, tm, tn):
    b, l, d = x.shape
    n = w.shape[1]
    return pl.pallas_call(
        _norm_matmul_kernel,
        grid=(b, l // tm, n // tn),
        in_specs=[
            pl.BlockSpec((1, tm, d), lambda bi, i, j: (bi, i, 0)),
            pl.BlockSpec((1, d), lambda bi, i, j: (0, 0)),
            pl.BlockSpec((1, 1, d), lambda bi, i, j: (bi, 0, 0)),
            pl.BlockSpec((1, 1, d), lambda bi, i, j: (bi, 0, 0)),
            pl.BlockSpec((d, tn), lambda bi, i, j: (0, j)),
        ],
        out_specs=pl.BlockSpec((1, tm, tn), lambda bi, i, j: (bi, i, j)),
        out_shape=jax.ShapeDtypeStruct((b, l, n), bf16),
        scratch_shapes=[pltpu.VMEM((tm, d), bf16)],
        compiler_params=_cparams(3),
        name="norm_matmul",
    )(x, g.reshape(1, d), sc, sh, w)


def _hgrn_kernel(q_ref, ff_ref, fb_ref, v_ref, g_ref, cff_ref, cfb_ref, cv_ref, lb_ref, ng_ref,
                 a_ref, of_ref, ob_ref, qd_ref, kd_ref, ks_ref, dec_ref, st_ref):
    C = SCAN_CHUNK
    G = GATE_ROWS
    L = q_ref.shape[1]
    Lc = cff_ref.shape[1]
    n = L // C
    row = lax.broadcasted_iota(i32, (C, C), 0)
    col = lax.broadcasted_iota(i32, (C, C), 1)
    lower = col <= row
    upper = col >= row
    grow = lax.broadcasted_iota(i32, (G, G), 0)
    gcol = lax.broadcasted_iota(i32, (G, G), 1)
    same_chunk = (grow // C) == (gcol // C)
    tri = ((same_chunk & (gcol <= grow)).astype(bf16), (same_chunk & (gcol >= grow)).astype(bf16))
    lbs = (lb_ref[0:1, :], lb_ref[1:2, :])

    def gate_block(f_raw, d):
        f = lbs[d] + (1.0 - lbs[d]) * jax.nn.sigmoid(f_raw.astype(f32))
        logf = jnp.log(f)
        hi = logf.astype(bf16)
        rem = logf - hi.astype(f32)
        mid = rem.astype(bf16)
        low = (rem - mid.astype(f32)).astype(bf16)
        cum = (jnp.dot(tri[d], hi, preferred_element_type=f32) + jnp.dot(tri[d], mid, preferred_element_type=f32)
               + jnp.dot(tri[d], low, preferred_element_type=f32))
        k3 = (1.0 - f).reshape(G // C, C, HEAD_DIM)
        cum3 = cum.reshape(G // C, C, HEAD_DIM)
        edge3 = cum3[:, C - 1:C, :] if d == 0 else cum3[:, 0:1, :]
        k_dec = (k3 * jnp.exp(-cum3)).reshape(G, HEAD_DIM).astype(bf16)
        k_st = (k3 * jnp.exp(edge3 - cum3)).reshape(G, HEAD_DIM).astype(bf16)
        dec = jnp.broadcast_to(jnp.exp(edge3), (G // C, C, HEAD_DIM)).reshape(G, HEAD_DIM)
        return cum, k_dec, k_st, dec

    f_refs = (cff_ref, cfb_ref)
    for d in range(2):
        st = jnp.zeros((HEAD_DIM, HEAD_DIM), f32)
        blocks = range(Lc // G) if d == 0 else range(Lc // G - 1, -1, -1)
        for gb in blocks:
            _, _, k_st, dec = gate_block(f_refs[d][0, gb * G:(gb + 1) * G, :], d)
            chunks = range(G // C) if d == 0 else range(G // C - 1, -1, -1)
            for c in chunks:
                v = cv_ref[0, gb * G + c * C:gb * G + (c + 1) * C, :]
                st = (dec[c * C:c * C + 1, :] * st
                      + lax.dot_general(v, k_st[c * C:(c + 1) * C, :], TN, preferred_element_type=f32))
        st_ref[d] = st

    def precompute(i, carry):
        rows = pl.ds(pl.multiple_of(i * G, G), G)
        qs = _silu(q_ref[0, rows, :].astype(f32))
        for d, f_ref in enumerate((ff_ref, fb_ref)):
            cum, k_dec, k_st, dec = gate_block(f_ref[0, rows, :], d)
            qd_ref[d, rows, :] = (qs * jnp.exp(cum)).astype(bf16)
            kd_ref[d, rows, :] = k_dec
            ks_ref[d, rows, :] = k_st
            dec_ref[d, rows, :] = dec
        return carry

    lax.fori_loop(0, L // G, precompute, 0)

    def chunk_step(off, d, mask, o_ref):
        rows = pl.ds(off, C)
        qd = qd_ref[d, rows, :]
        v = v_ref[0, rows, :]
        s = lax.dot_general(qd, kd_ref[d, rows, :], NT, preferred_element_type=f32)
        s = jnp.where(mask, s, 0.0).astype(bf16)
        st = st_ref[d]
        o_ref[rows, :] = (jnp.dot(s, v, preferred_element_type=f32)
                          + lax.dot_general(qd, st.astype(bf16), NT, preferred_element_type=f32))
        st_ref[d] = (dec_ref[d, pl.ds(off, 1), :] * st
                     + lax.dot_general(v, ks_ref[d, rows, :], TN, preferred_element_type=f32))

    def body(c, carry):
        chunk_step(pl.multiple_of(c * C, C), 0, lower, of_ref)
        chunk_step(pl.multiple_of((n - 1 - c) * C, C), 1, upper, ob_ref)
        return carry

    lax.fori_loop(0, n, body, 0, unroll=2)

    R = 512 if L % 512 == 0 else C

    def epilogue(r, carry):
        rows = pl.ds(pl.multiple_of(r * R, R), R)
        o = of_ref[rows, :] + ob_ref[rows, :]
        o = o * lax.rsqrt(jnp.mean(o * o, axis=-1, keepdims=True) + EPS)
        o = o * ng_ref[...]
        a_ref[0, rows, :] = (o * _silu(g_ref[0, rows, :].astype(f32))).astype(a_ref.dtype)
        return carry

    lax.fori_loop(0, L // R, epilogue, 0)


def _hgrn(proj, proj_c, lb, norm_g, width_a):
    b, l, _ = proj.shape
    lc = proj_c.shape[1]
    nh = width_a // HEAD_DIM

    def seg(k):
        return pl.BlockSpec((1, l, HEAD_DIM), lambda bi, h, k=k: (bi, 0, k * nh + h))

    def cseg(k):
        return pl.BlockSpec((1, lc, HEAD_DIM), lambda bi, h, k=k: (bi, 0, k * nh + h))

    return pl.pallas_call(
        _hgrn_kernel,
        grid=(b, nh),
        in_specs=[seg(0), seg(1), seg(2), seg(3), seg(4), cseg(0), cseg(1), cseg(2),
                  pl.BlockSpec((2, HEAD_DIM), lambda bi, h: (0, h)),
                  pl.BlockSpec((1, HEAD_DIM), lambda bi, h: (0, h))],
        out_specs=pl.BlockSpec((1, l, HEAD_DIM), lambda bi, h: (bi, 0, h)),
        out_shape=jax.ShapeDtypeStruct((b, l, width_a), bf16),
        scratch_shapes=[pltpu.VMEM((l, HEAD_DIM), f32), pltpu.VMEM((l, HEAD_DIM), f32),
                        pltpu.VMEM((2, l, HEAD_DIM), bf16), pltpu.VMEM((2, l, HEAD_DIM), bf16),
                        pltpu.VMEM((2, l, HEAD_DIM), bf16), pltpu.VMEM((2, l, HEAD_DIM), f32),
                        pltpu.VMEM((2, HEAD_DIM, HEAD_DIM), f32)],
        compiler_params=_cparams(2),
        name="hgrn",
    )(proj, proj, proj, proj, proj, proj_c, proj_c, proj_c, lb, norm_g.reshape(1, width_a))


def _residual_norm_route(x, y, gt, g2, sc, sh, wr_ref, br_ref, x1_ref, h2_ref, route_ref, cnt_ref, cnt_acc):
    @pl.when((pl.program_id(0) == 0) & (pl.program_id(1) == 0))
    def _():
        cnt_acc[...] = jnp.zeros_like(cnt_acc)

    x1 = x + gt * y
    x1_ref[0] = x1
    h2 = _rms_mod(x1, g2, sc, sh)
    h2_ref[0] = h2
    logits = jnp.dot(h2.astype(bf16), wr_ref[...], preferred_element_type=f32) + br_ref[...]
    tm = logits.shape[0]
    lane = lax.broadcasted_iota(i32, (tm, ROUTE_LANES), 1)
    lane_f = lane.astype(f32)
    neg = -jnp.inf
    gl = jnp.where(lane < N_EXPERT_GROUPS, logits, neg)
    gmax = jnp.max(gl, axis=-1, keepdims=True)
    p_top = 1.0 / jnp.sum(jnp.exp(gl - gmax), axis=-1, keepdims=True)
    grp = jnp.min(jnp.where(gl == gmax, lane_f, float(ROUTE_LANES)), axis=-1, keepdims=True)
    lo = float(N_EXPERT_GROUPS) + grp * float(EXPERTS_PER_GROUP)
    emask = (lane_f >= lo) & (lane_f < lo + float(EXPERTS_PER_GROUP))
    el = jnp.where(emask, logits, neg)
    emax = jnp.max(el, axis=-1, keepdims=True)
    esum = jnp.sum(jnp.exp(el - emax), axis=-1, keepdims=True)
    i1 = jnp.min(jnp.where(el == emax, lane_f, float(ROUTE_LANES)), axis=-1, keepdims=True)
    p1 = 1.0 / esum
    el2 = jnp.where(lane_f == i1, neg, el)
    emax2 = jnp.max(el2, axis=-1, keepdims=True)
    i2 = jnp.min(jnp.where(el2 == emax2, lane_f, float(ROUTE_LANES)), axis=-1, keepdims=True)
    p2 = jnp.exp(emax2 - emax) / esum
    den = p1 + p2
    w1 = p_top * p1 / den
    w2 = p_top * p2 / den
    e1 = i1 - float(N_EXPERT_GROUPS)
    e2 = i2 - float(N_EXPERT_GROUPS)
    hot1 = lane_f == i1
    hot2 = lane_f == i2
    both = jnp.where(hot1 | hot2, 1.0, 0.0)
    trow = lax.broadcasted_iota(i32, (tm, tm), 0)
    tcol = lax.broadcasted_iota(i32, (tm, tm), 1)
    before = (tcol < trow).astype(bf16)
    base = jnp.dot(before, both.astype(bf16), preferred_element_type=f32) + cnt_acc[...]
    r1 = jnp.sum(jnp.where(hot1, base, 0.0), axis=-1, keepdims=True)
    r2 = jnp.sum(jnp.where(hot2, base, 0.0), axis=-1, keepdims=True)
    cnt = cnt_acc[...] + jnp.sum(both, axis=0, keepdims=True)
    cnt_acc[...] = cnt
    cnt_ref[...] = cnt
    route = jnp.zeros((tm, ROUTE_LANES), f32)
    for k, val in enumerate((e1, e2, w1, w2, r1, r2)):
        route = jnp.where(lane == k, val, route)
    route_ref[0] = route


def _gelu_tanh(v):
    return 0.5 * v * (1.0 + jnp.tanh(0.7978845608028654 * (v + 0.044715 * (v * v * v))))


def _even_out_kernel(x_ref, a_ref, u_ref, v_ref, lng_ref, lnb_ref, ws_ref, bs_ref, wo_ref, gt_ref,
                     g2_ref, sc_ref, sh_ref, wr_ref, br_ref, x1_ref, h2_ref, route_ref, cnt_ref, bm_ref, cnt_acc):
    tm = x_ref.shape[1]
    wb = u_ref.shape[2]
    gd = wb // N_GROUPS_B
    gv = _gelu_tanh(v_ref[0].astype(f32))
    mu = jnp.mean(gv, axis=-1, keepdims=True)
    dv = gv - mu
    var = jnp.mean(dv * dv, axis=-1, keepdims=True)
    vn = (dv * lax.rsqrt(var + EPS) * lng_ref[...] + lnb_ref[...]).astype(bf16)
    for nb in range(tm // CHUNK_B):
        rs = slice(nb * CHUNK_B, (nb + 1) * CHUNK_B)
        for g in range(N_GROUPS_B):
            cs = slice(g * gd, (g + 1) * gd)
            s = jnp.dot(ws_ref[g], vn[rs, cs], preferred_element_type=f32) + bs_ref[:, g:g + 1]
            bm_ref[rs, cs] = (_gelu_tanh(u_ref[0, rs, cs].astype(f32)) * s).astype(bf16)
    wa = a_ref.shape[2]
    y = (jnp.dot(a_ref[0], wo_ref[0:wa, :], preferred_element_type=f32)
         + jnp.dot(bm_ref[...], wo_ref[wa:, :], preferred_element_type=f32))
    _residual_norm_route(x_ref[0], y, gt_ref[0], g2_ref[...], sc_ref[0], sh_ref[0], wr_ref, br_ref,
                         x1_ref, h2_ref, route_ref, cnt_ref, cnt_acc)


def _tail_specs(d, tm):
    vec = pl.BlockSpec((1, d), lambda bi, i: (0, 0))
    per_b = pl.BlockSpec((1, 1, d), lambda bi, i: (bi, 0, 0))
    in_specs = [per_b, vec, per_b, per_b,
                pl.BlockSpec((d, ROUTE_LANES), lambda bi, i: (0, 0)),
                pl.BlockSpec((1, ROUTE_LANES), lambda bi, i: (0, 0))]
    out_specs = [pl.BlockSpec((1, tm, d), lambda bi, i: (bi, i, 0)),
                 pl.BlockSpec((1, tm, d), lambda bi, i: (bi, i, 0)),
                 pl.BlockSpec((1, tm, ROUTE_LANES), lambda bi, i: (bi, i, 0)),
                 pl.BlockSpec((1, ROUTE_LANES), lambda bi, i: (0, 0))]
    return in_specs, out_specs


def _tail_out_shapes(b, l, d):
    return [jax.ShapeDtypeStruct((b, l, d), f32), jax.ShapeDtypeStruct((b, l, d), f32),
            jax.ShapeDtypeStruct((b, l, ROUTE_LANES), f32), jax.ShapeDtypeStruct((1, ROUTE_LANES), f32)]


def _even_out(x, a, proj, ln_g, ln_b, w_s, b_s_t, w_out, gt, g2, sc, sh, wr, br, tm):
    b, l, d = x.shape
    wa = a.shape[2]
    wb = ln_g.shape[0]
    ub = (5 * wa) // wb
    tail_in, tail_out = _tail_specs(d, tm)
    return pl.pallas_call(
        _even_out_kernel,
        grid=(b, l // tm),
        in_specs=[
            pl.BlockSpec((1, tm, d), lambda bi, i: (bi, i, 0)),
            pl.BlockSpec((1, tm, wa), lambda bi, i: (bi, i, 0)),
            pl.BlockSpec((1, tm, wb), lambda bi, i: (bi, i, ub)),
            pl.BlockSpec((1, tm, wb), lambda bi, i: (bi, i, ub + 1)),
            pl.BlockSpec((1, wb), lambda bi, i: (0, 0)),
            pl.BlockSpec((1, wb), lambda bi, i: (0, 0)),
            pl.BlockSpec((N_GROUPS_B, CHUNK_B, CHUNK_B), lambda bi, i: (0, 0, 0)),
            pl.BlockSpec((CHUNK_B, N_GROUPS_B), lambda bi, i: (0, 0)),
            pl.BlockSpec((wa + wb, d), lambda bi, i: (0, 0)),
        ] + tail_in,
        out_specs=tail_out,
        out_shape=_tail_out_shapes(b, l, d),
        scratch_shapes=[pltpu.VMEM((tm, wb), bf16), pltpu.VMEM((1, ROUTE_LANES), f32)],
        compiler_params=_cparams(2),
        name="even_out",
    )(x, a, proj, proj, ln_g.reshape(1, wb), ln_b.reshape(1, wb), w_s, b_s_t, w_out, gt, g2.reshape(1, d), sc, sh, wr, br)


def _odd_out_kernel(hx_ref, bg_ref, cg_ref, cw_ref, x_ref, wo_ref, gt_ref,
                    g2_ref, sc_ref, sh_ref, wr_ref, br_ref, x1_ref, h2_ref, route_ref, cnt_ref, cnt_acc):
    tm = x_ref.shape[1]
    z = cg_ref[0].astype(f32) * hx_ref[0].astype(f32)
    pos = lax.broadcasted_iota(i32, (tm, 1), 0) % GRID_W
    z_prev = jnp.where(pos == 0, 0.0, pltpu.roll(z, 1, 0))
    z_next = jnp.where(pos == GRID_W - 1, 0.0, pltpu.roll(z, tm - 1, 0))
    conv = z_prev * cw_ref[0:1, :] + z * cw_ref[1:2, :] + z_next * cw_ref[2:3, :]
    t = (bg_ref[0].astype(f32) * conv).astype(bf16)
    y = jnp.dot(t, wo_ref[...], preferred_element_type=f32)
    _residual_norm_route(x_ref[0], y, gt_ref[0], g2_ref[...], sc_ref[0], sh_ref[0], wr_ref, br_ref,
                         x1_ref, h2_ref, route_ref, cnt_ref, cnt_acc)


def _odd_out(x, pc, conv_w, w_out, gt, g2, sc, sh, wr, br, tm):
    b, l, d = x.shape
    wc = w_out.shape[0]
    tail_in, tail_out = _tail_specs(d, tm)
    return pl.pallas_call(
        _odd_out_kernel,
        grid=(b, l // tm),
        in_specs=[
            pl.BlockSpec((1, tm, wc), lambda bi, i: (bi, i, 0)),
            pl.BlockSpec((1, tm, wc), lambda bi, i: (bi, i, 1)),
            pl.BlockSpec((1, tm, wc), lambda bi, i: (bi, i, 2)),
            pl.BlockSpec((conv_w.shape[0], wc), lambda bi, i: (0, 0)),
            pl.BlockSpec((1, tm, d), lambda bi, i: (bi, i, 0)),
            pl.BlockSpec((wc, d), lambda bi, i: (0, 0)),
        ] + tail_in,
        out_specs=tail_out,
        out_shape=_tail_out_shapes(b, l, d),
        scratch_shapes=[pltpu.VMEM((1, ROUTE_LANES), f32)],
        compiler_params=_cparams(2),
        name="odd_out",
    )(pc, pc, pc, conv_w, x, w_out, gt, g2.reshape(1, d), sc, sh, wr, br)


def _row_gather_start(idx_ref, n_rows, src_hbm, dst_buf, sem):
    def body(r, carry):
        t = idx_ref[0, 0, r]
        pltpu.make_async_copy(src_hbm.at[pl.ds(t, 1), :], dst_buf.at[pl.ds(r, 1), :], sem).start()
        return carry

    lax.fori_loop(0, n_rows, body, 0, unroll=8)


def _row_gather_wait(n_rows, src_hbm, dst_buf, sem):
    def body(r, carry):
        pltpu.make_async_copy(src_hbm.at[pl.ds(0, 1), :], dst_buf.at[pl.ds(r, 1), :], sem).wait()
        return carry

    lax.fori_loop(0, n_rows, body, 0, unroll=8)


def _dispatch_kernel(pos_ref, h_hbm, xs_hbm, sem):
    i = pl.program_id(0)
    n = pl.num_programs(0)
    tm = pos_ref.shape[2] // TOP_K
    slot = lax.rem(i, 2)

    def start(r, carry):
        src = h_hbm.at[pl.ds(i * tm + r, 1), :]
        for k in range(TOP_K):
            pltpu.make_async_copy(src, xs_hbm.at[pl.ds(pos_ref[0, 0, k * tm + r], 1), :], sem.at[slot]).start()
        return carry

    lax.fori_loop(0, tm, start, 0, unroll=8)

    def wait_all(s):
        def wait(r, carry):
            pltpu.make_async_copy(h_hbm.at[pl.ds(0, 1), :], xs_hbm.at[pl.ds(0, 1), :], sem.at[s]).wait()
            return carry

        lax.fori_loop(0, TOP_K * tm, wait, 0, unroll=8)

    @pl.when(i > 0)
    def _():
        wait_all(1 - slot)

    @pl.when(i == n - 1)
    def _():
        wait_all(slot)


def _pos_blocks(pos, tm):
    nt = pos.shape[0] // tm
    return pos.reshape(nt, tm, TOP_K).transpose(0, 2, 1).reshape(nt, 1, TOP_K * tm)


def _moe_dispatch(h2, pos, tm):
    t, d = h2.shape
    return pl.pallas_call(
        _dispatch_kernel,
        grid=(t // tm,),
        in_specs=[pl.BlockSpec((1, 1, TOP_K * tm), lambda i: (i, 0, 0), memory_space=pltpu.SMEM),
                  pl.BlockSpec(memory_space=pl.ANY)],
        out_specs=pl.BlockSpec(memory_space=pl.ANY),
        out_shape=jax.ShapeDtypeStruct((t * TOP_K, d), h2.dtype),
        scratch_shapes=[pltpu.SemaphoreType.DMA((2,))],
        compiler_params=_cparams(1),
        name="moe_dispatch",
    )(_pos_blocks(pos, tm), h2)


def _ffn_kernel(blk_ref, e_ref, lo_ref, hi_ref, x_ref, wg_ref, wu_ref, wd_ref, y_ref, wg_s, wu_s, wd_s):
    w = pl.program_id(0)
    lo = lo_ref[w]
    hi = hi_ref[w]
    prev = jnp.maximum(w - 1, 0)

    @pl.when(hi > lo)
    def _():
        @pl.when((w == 0) | (e_ref[w] != e_ref[prev]))
        def _():
            wg_s[...] = wg_ref[0, 0].astype(bf16)
            wu_s[...] = wu_ref[0, 0].astype(bf16)
            wd_s[...] = wd_ref[0, 0].astype(bf16)

        xb = x_ref[...].astype(bf16)
        gate = jnp.dot(xb, wg_s[...], preferred_element_type=f32)
        up = jnp.dot(xb, wu_s[...], preferred_element_type=f32)
        mid = (_silu(gate) * up).astype(bf16)
        y = jnp.dot(mid, wd_s[...], preferred_element_type=f32)
        r = lax.broadcasted_iota(i32, (y.shape[0], 1), 0)
        mine = (r >= lo) & (r < hi)
        first_visit = (w == 0) | (blk_ref[w] != blk_ref[prev])

        @pl.when(first_visit)
        def _():
            y_ref[...] = jnp.where(mine, y, 0.0)

        @pl.when(jnp.logical_not(first_visit))
        def _():
            y_ref[...] = jnp.where(mine, y, y_ref[...])


def _moe_ffn(xs, items, layer, w_gate, w_up, w_down):
    rows, d = xs.shape
    bm = MOE_BLOCK
    ff = w_gate.shape[3]
    n_items = items[0].shape[0]
    grid_spec = pltpu.PrefetchScalarGridSpec(
        num_scalar_prefetch=4,
        grid=(n_items,),
        in_specs=[
            pl.BlockSpec((bm, d), lambda w, blk, e, lo, hi: (blk[w], 0)),
            pl.BlockSpec((1, 1, d, ff), lambda w, blk, e, lo, hi: (layer, e[w], 0, 0)),
            pl.BlockSpec((1, 1, d, ff), lambda w, blk, e, lo, hi: (layer, e[w], 0, 0)),
            pl.BlockSpec((1, 1, ff, d), lambda w, blk, e, lo, hi: (layer, e[w], 0, 0)),
        ],
        out_specs=pl.BlockSpec((bm, d), lambda w, blk, e, lo, hi: (blk[w], 0)),
        scratch_shapes=[pltpu.VMEM((d, ff), bf16), pltpu.VMEM((d, ff), bf16), pltpu.VMEM((ff, d), bf16)],
    )
    return pl.pallas_call(
        _ffn_kernel,
        grid_spec=grid_spec,
        out_shape=jax.ShapeDtypeStruct((rows, d), f32),
        compiler_params=_cparams(1),
        name="moe_ffn",
    )(*items, xs, w_gate, w_up, w_down)


def _combine_kernel(pos_cur_ref, pos_nxt_ref, x_ref, gt_ref, fg_ref, route_ref, ys_hbm, o_ref, ybuf, sem, *, final):
    i = pl.program_id(0)
    n = pl.num_programs(0)
    tm = x_ref.shape[1]
    slot = lax.rem(i, 2)

    @pl.when(i == 0)
    def _():
        _row_gather_start(pos_cur_ref, TOP_K * tm, ys_hbm, ybuf.at[0], sem.at[0])

    @pl.when(i + 1 < n)
    def _():
        _row_gather_start(pos_nxt_ref, TOP_K * tm, ys_hbm, ybuf.at[1 - slot], sem.at[1 - slot])

    _row_gather_wait(TOP_K * tm, ys_hbm, ybuf.at[slot], sem.at[slot])
    route = route_ref[0]
    f = ybuf[slot, 0:tm, :] * route[:, TOP_K:TOP_K + 1] + ybuf[slot, tm:2 * tm, :] * route[:, TOP_K + 1:TOP_K + 2]
    xo = x_ref[0] + gt_ref[0] * f
    if final:
        xo = (xo * lax.rsqrt(jnp.mean(xo * xo, axis=-1, keepdims=True) + EPS)) * fg_ref[...]
    o_ref[0] = xo


def _moe_combine(x1, gt, final_g, route, ys, pos, tm, final):
    b, l, d = x1.shape
    nt = (b * l) // tm
    per_b = l // tm
    pos3 = _pos_blocks(pos, tm)
    return pl.pallas_call(
        functools.partial(_combine_kernel, final=final),
        grid=(nt,),
        in_specs=[
            pl.BlockSpec((1, 1, TOP_K * tm), lambda i: (i, 0, 0), memory_space=pltpu.SMEM),
            pl.BlockSpec((1, 1, TOP_K * tm), lambda i: (jnp.minimum(i + 1, nt - 1), 0, 0), memory_space=pltpu.SMEM),
            pl.BlockSpec((1, tm, d), lambda i: (i // per_b, i % per_b, 0)),
            pl.BlockSpec((1, 1, d), lambda i: (i // per_b, 0, 0)),
            pl.BlockSpec((1, d), lambda i: (0, 0)),
            pl.BlockSpec((1, tm, ROUTE_LANES), lambda i: (i // per_b, i % per_b, 0)),
            pl.BlockSpec(memory_space=pl.ANY),
        ],
        out_specs=pl.BlockSpec((1, tm, d), lambda i: (i // per_b, i % per_b, 0)),
        out_shape=jax.ShapeDtypeStruct((b, l, d), f32),
        scratch_shapes=[pltpu.VMEM((2, TOP_K * tm, d), f32), pltpu.SemaphoreType.DMA((2,))],
        compiler_params=_cparams(1),
        name="moe_combine",
    )(pos3, pos3, x1, gt, final_g.reshape(1, d), route, ys)


def _dispatch_plan(route, cnt, n_tok):
    bm = MOE_BLOCK
    n_rows = n_tok * TOP_K
    n_blk = n_rows // bm
    experts = jnp.arange(N_EXPERTS, dtype=i32)
    counts = cnt[0, N_EXPERT_GROUPS:N_EXPERT_GROUPS + N_EXPERTS].astype(i32)
    ends = jnp.cumsum(counts)
    starts = ends - counts
    eid = route[..., 0:TOP_K].astype(i32).reshape(n_tok, TOP_K)
    rank = route[..., 2 * TOP_K:3 * TOP_K].astype(i32).reshape(n_tok, TOP_K)
    pos = rank + jnp.sum(jnp.where(eid[..., None] == experts, starts, 0), axis=-1)
    first_blk = starts // bm
    n_it = jnp.where(counts > 0, (ends - 1) // bm - first_blk + 1, 0)
    it_end = jnp.cumsum(n_it)
    it_start = it_end - n_it
    w = jnp.arange(n_blk + N_EXPERTS, dtype=i32)
    live = w < it_end[-1]
    e_last = jnp.max(jnp.where(counts > 0, experts, 0))
    e_w = jnp.where(live, jnp.sum((it_end[None, :] <= w[:, None]).astype(i32), axis=1), e_last)
    hot = e_w[:, None] == experts
    pick = lambda tab: jnp.sum(jnp.where(hot, tab, 0), axis=1)
    blk_w = jnp.where(live, pick(first_blk) + w - pick(it_start), n_blk - 1)
    lo_w = jnp.where(live, jnp.maximum(pick(starts), blk_w * bm) - blk_w * bm, 0)
    hi_w = jnp.where(live, jnp.minimum(pick(ends), (blk_w + 1) * bm) - blk_w * bm, 0)
    return pos, (blk_w, e_w, lo_w, hi_w)


def _hier_moe(x1, h2, route, cnt, gt, final_g, layer, w_gate, w_up, w_down, final):
    b, l, d = x1.shape
    n_tok = b * l
    assert (n_tok * TOP_K) % MOE_BLOCK == 0
    pos, items = _dispatch_plan(route, cnt, n_tok)
    xs = _moe_dispatch(h2.reshape(n_tok, d), pos, 512)
    ys = _moe_ffn(xs, items, layer, w_gate, w_up, w_down)
    return _moe_combine(x1, gt, final_g, route, ys, pos, 256, final)


def _router_params(rg_w, rg_b, re_w, re_b):
    d = rg_w.shape[0]
    pad = ROUTE_LANES - N_EXPERT_GROUPS - N_EXPERTS
    wr = jnp.concatenate([rg_w, re_w, jnp.zeros((d, pad), f32)], axis=1).astype(bf16)
    br = jnp.concatenate([rg_b, re_b, jnp.zeros((pad,), f32)]).reshape(1, ROUTE_LANES)
    return wr, br


def kernel(x, c, ctx, c_ctx, w_mod, b_mod, norm_mix_g, norm_ffn_g, final_g, even_w_in, even_w_out, hgrn_lb_raw,
           hgrn_norm_g, gmlp_ln_g, gmlp_ln_b, gmlp_w_s, gmlp_b_s, odd_w_in, odd_conv_w, odd_w_out, router_g_w,
           router_g_b, router_e_w, router_e_b, exp_w_gate, exp_w_up, exp_w_down):
    b, l, d = x.shape
    depth = w_mod.shape[0]
    assert depth == 2, "layer plan below is written for one even and one odd layer"
    width_a = hgrn_norm_g.shape[1]
    lc = ctx.shape[1]

    rows = -(-(b + 1) // 8) * 8
    c_all = jnp.zeros((rows, d), f32).at[:b].set(c).at[b].set(c_ctx)
    mod = _mod(c_all, w_mod, b_mod)

    def latent_mod(layer):
        return [m.reshape(b, 1, d) for m in jnp.split(mod[layer, :b], 6, axis=-1)]

    sh_m, sc_m, gt_m, sh_f, sc_f, gt_f = latent_mod(0)
    csh_m = jnp.broadcast_to(mod[0, b, 0:d].reshape(1, 1, d), (b, 1, d))
    csc_m = jnp.broadcast_to(mod[0, b, d:2 * d].reshape(1, 1, d), (b, 1, d))
    w_in = even_w_in[0].astype(bf16)
    proj = _norm_matmul(x, norm_mix_g[0], sc_m, sh_m, w_in, 512, 1024)
    proj_c = _norm_matmul(ctx, norm_mix_g[0], csc_m, csh_m, w_in[:, width_a:4 * width_a], lc, 1024)
    lb = jnp.cumsum(jax.nn.softmax(hgrn_lb_raw.astype(f32), axis=0), axis=0)[0]
    a = _hgrn(proj, proj_c, lb, hgrn_norm_g[0], width_a)
    wr, br = _router_params(router_g_w[0], router_g_b[0], router_e_w[0], router_e_b[0])
    x1, h2, route, cnt = _even_out(x, a, proj, gmlp_ln_g[0], gmlp_ln_b[0], gmlp_w_s[0].astype(bf16), gmlp_b_s[0].T,
                                   even_w_out[0].astype(bf16), gt_m, norm_ffn_g[0], sc_f, sh_f, wr, br, 512)
    x = _hier_moe(x1, h2, route, cnt, gt_f, final_g, 0, exp_w_gate, exp_w_up, exp_w_down, False)

    sh_m, sc_m, gt_m, sh_f, sc_f, gt_f = latent_mod(1)
    pc = _norm_matmul(x, norm_mix_g[1], sc_m, sh_m, odd_w_in[0].astype(bf16), 512, 1024)
    wr, br = _router_params(router_g_w[1], router_g_b[1], router_e_w[1], router_e_b[1])
    x1, h2, route, cnt = _odd_out(x, pc, odd_conv_w[0], odd_w_out[0].astype(bf16), gt_m, norm_ffn_g[1], sc_f, sh_f,
                                  wr, br, 512)
    return _hier_moe(x1, h2, route, cnt, gt_f, final_g, 1, exp_w_gate, exp_w_up, exp_w_down, True)
```

```python
import functools

import jax
import jax.numpy as jnp
from jax import lax
from jax.experimental import pallas as pl
from jax.experimental.pallas import tpu as pltpu

f32 = jnp.float32
bf16 = jnp.bfloat16
i32 = jnp.int32

EPS = 1e-6
HEAD_DIM = 128
SCAN_CHUNK = 64
GATE_ROWS = 128
CHUNK_B = 128
N_GROUPS_B = 8
GRID_W = 64
N_EXPERT_GROUPS = 4
EXPERTS_PER_GROUP = 8
N_EXPERTS = N_EXPERT_GROUPS * EXPERTS_PER_GROUP
TOP_K = 2
ROUTE_LANES = 128
MOE_BLOCK = 256
VMEM_LIMIT = 56 * 1024 * 1024

NT = (((1,), (1,)), ((), ()))
TN = (((0,), (0,)), ((), ()))


def _cparams(n_axes):
    return pltpu.CompilerParams(dimension_semantics=("arbitrary",) * n_axes, vmem_limit_bytes=VMEM_LIMIT)


def _silu(v):
    return v * jax.nn.sigmoid(v)


def _mod_kernel(c_ref, w_ref, b_ref, o_ref):
    s = _silu(c_ref[...])
    o_ref[0] = jnp.dot(s, w_ref[0], preferred_element_type=f32, precision=lax.Precision.HIGHEST) + b_ref[0]


def _mod(c_all, w_mod, b_mod):
    depth, d, n = w_mod.shape
    rows = c_all.shape[0]
    tn = 1024
    return pl.pallas_call(
        _mod_kernel,
        grid=(depth, n // tn),
        in_specs=[
            pl.BlockSpec((rows, d), lambda l, j: (0, 0)),
            pl.BlockSpec((1, d, tn), lambda l, j: (l, 0, j)),
            pl.BlockSpec((1, 1, tn), lambda l, j: (l, 0, j)),
        ],
        out_specs=pl.BlockSpec((1, rows, tn), lambda l, j: (l, 0, j)),
        out_shape=jax.ShapeDtypeStruct((depth, rows, n), f32),
        compiler_params=_cparams(2),
        name="mod",
    )(c_all, w_mod, b_mod.reshape(depth, 1, n))


def _rms_mod(x, g, sc, sh):
    y = x * lax.rsqrt(jnp.mean(x * x, axis=-1, keepdims=True) + EPS)
    return (y * g) * (1.0 + sc) + sh


def _norm_matmul_kernel(x_ref, g_ref, sc_ref, sh_ref, w_ref, o_ref, h_ref):
    @pl.when(pl.program_id(2) == 0)
    def _():
        h_ref[...] = _rms_mod(x_ref[0], g_ref[...], sc_ref[0], sh_ref[0]).astype(bf16)

    o_ref[0] = jnp.dot(h_ref[...], w_ref[...], preferred_element_type=f32).astype(o_ref.dtype)


def _norm_matmul(x, g, sc, sh, w, tm, tn):
    b, l, d = x.shape
    n = w.shape[1]
    return pl.pallas_call(
        _norm_matmul_kernel,
        grid=(b, l // tm, n // tn),
        in_specs=[
            pl.BlockSpec((1, tm, d), lambda bi, i, j: (bi, i, 0)),
            pl.BlockSpec((1, d), lambda bi, i, j: (0, 0)),
            pl.BlockSpec((1, 1, d), lambda bi, i, j: (bi, 0, 0)),
            pl.BlockSpec((1, 1, d), lambda bi, i, j: (bi, 0, 0)),
            pl.BlockSpec((d, tn), lambda bi, i, j: (0, j)),
        ],
        out_specs=pl.BlockSpec((1, tm, tn), lambda bi, i, j: (bi, i, j)),
        out_shape=jax.ShapeDtypeStruct((b, l, n), bf16),
        scratch_shapes=[pltpu.VMEM((tm, d), bf16)],
        compiler_params=_cparams(3),
        name="norm_matmul",
    )(x, g.reshape(1, d), sc, sh, w)


def _matmul_kernel(x_ref, w_ref, o_ref):
    o_ref[...] = jnp.dot(x_ref[...], w_ref[...], preferred_element_type=f32).astype(o_ref.dtype)


def _matmul(x, w, tm, tn):
    m, d = x.shape
    n = w.shape[1]
    tm = min(tm, m)
    return pl.pallas_call(
        _matmul_kernel,
        grid=(m // tm, n // tn),
        in_specs=[pl.BlockSpec((tm, d), lambda i, j: (i, 0)), pl.BlockSpec((d, tn), lambda i, j: (0, j))],
        out_specs=pl.BlockSpec((tm, tn), lambda i, j: (i, j)),
        out_shape=jax.ShapeDtypeStruct((m, n), bf16),
        compiler_params=_cparams(2),
        name="matmul",
    )(x, w)


def _hgrn_kernel(q_ref, ff_ref, fb_ref, v_ref, g_ref, cff_ref, cfb_ref, cv_ref, lb_ref, ng_ref,
                 a_ref, of_ref, ob_ref, qd_ref, kd_ref, ks_ref, dec_ref, st_ref):
    C = SCAN_CHUNK
    G = GATE_ROWS
    L = q_ref.shape[1]
    Lc = cff_ref.shape[1]
    n = L // C
    row = lax.broadcasted_iota(i32, (C, C), 0)
    col = lax.broadcasted_iota(i32, (C, C), 1)
    lower = col <= row
    upper = col >= row
    grow = lax.broadcasted_iota(i32, (G, G), 0)
    gcol = lax.broadcasted_iota(i32, (G, G), 1)
    same_chunk = (grow // C) == (gcol // C)
    tri = ((same_chunk & (gcol <= grow)).astype(bf16), (same_chunk & (gcol >= grow)).astype(bf16))
    lbs = (lb_ref[0:1, :], lb_ref[1:2, :])

    def gate_block(f_raw, d):
        f = lbs[d] + (1.0 - lbs[d]) * jax.nn.sigmoid(f_raw.astype(f32))
        logf = jnp.log(f)
        hi = logf.astype(bf16)
        rem = logf - hi.astype(f32)
        mid = rem.astype(bf16)
        low = (rem - mid.astype(f32)).astype(bf16)
        cum = (jnp.dot(tri[d], hi, preferred_element_type=f32) + jnp.dot(tri[d], mid, preferred_element_type=f32)
               + jnp.dot(tri[d], low, preferred_element_type=f32))
        k3 = (1.0 - f).reshape(G // C, C, HEAD_DIM)
        cum3 = cum.reshape(G // C, C, HEAD_DIM)
        edge3 = cum3[:, C - 1:C, :] if d == 0 else cum3[:, 0:1, :]
        k_dec = (k3 * jnp.exp(-cum3)).reshape(G, HEAD_DIM).astype(bf16)
        k_st = (k3 * jnp.exp(edge3 - cum3)).reshape(G, HEAD_DIM).astype(bf16)
        dec = jnp.broadcast_to(jnp.exp(edge3), (G // C, C, HEAD_DIM)).reshape(G, HEAD_DIM)
        return cum, k_dec, k_st, dec

    f_refs = (cff_ref, cfb_ref)
    for d in range(2):
        st = jnp.zeros((HEAD_DIM, HEAD_DIM), f32)
        blocks = range(Lc // G) if d == 0 else range(Lc // G - 1, -1, -1)
        for gb in blocks:
            _, _, k_st, dec = gate_block(f_refs[d][0, gb * G:(gb + 1) * G, :], d)
            chunks = range(G // C) if d == 0 else range(G // C - 1, -1, -1)
            for c in chunks:
                v = cv_ref[0, gb * G + c * C:gb * G + (c + 1) * C, :]
                st = (dec[c * C:c * C + 1, :] * st
                      + lax.dot_general(v, k_st[c * C:(c + 1) * C, :], TN, preferred_element_type=f32))
        st_ref[d] = st

    def precompute(i, carry):
        rows = pl.ds(pl.multiple_of(i * G, G), G)
        qs = _silu(q_ref[0, rows, :].astype(f32))
        for d, f_ref in enumerate((ff_ref, fb_ref)):
            cum, k_dec, k_st, dec = gate_block(f_ref[0, rows, :], d)
            qd_ref[d, rows, :] = (qs * jnp.exp(cum)).astype(bf16)
            kd_ref[d, rows, :] = k_dec
            ks_ref[d, rows, :] = k_st
            dec_ref[d, rows, :] = dec
        return carry

    lax.fori_loop(0, L // G, precompute, 0)

    def chunk_step(off, d, mask, o_ref):
        rows = pl.ds(off, C)
        qd = qd_ref[d, rows, :]
        v = v_ref[0, rows, :]
        s = lax.dot_general(qd, kd_ref[d, rows, :], NT, preferred_element_type=f32)
        s = jnp.where(mask, s, 0.0).astype(bf16)
        st = st_ref[d]
        o_ref[rows, :] = (jnp.dot(s, v, preferred_element_type=f32)
                          + lax.dot_general(qd, st.astype(bf16), NT, preferred_element_type=f32))
        st_ref[d] = (dec_ref[d, pl.ds(off, 1), :] * st
                     + lax.dot_general(v, ks_ref[d, rows, :], TN, preferred_element_type=f32))

    def body(c, carry):
        chunk_step(pl.multiple_of(c * C, C), 0, lower, of_ref)
        chunk_step(pl.multiple_of((n - 1 - c) * C, C), 1, upper, ob_ref)
        return carry

    lax.fori_loop(0, n, body, 0, unroll=4)

    R = 512 if L % 512 == 0 else C

    def epilogue(r, carry):
        rows = pl.ds(pl.multiple_of(r * R, R), R)
        o = of_ref[rows, :] + ob_ref[rows, :]
        o = o * lax.rsqrt(jnp.mean(o * o, axis=-1, keepdims=True) + EPS)
        o = o * ng_ref[...]
        a_ref[0, rows, :] = (o * _silu(g_ref[0, rows, :].astype(f32))).astype(a_ref.dtype)
        return carry

    lax.fori_loop(0, L // R, epilogue, 0)


def _hgrn(proj, proj_c, lb, norm_g, width_a):
    b, l, _ = proj.shape
    lc = proj_c.shape[1]
    nh = width_a // HEAD_DIM

    def seg(k):
        return pl.BlockSpec((1, l, HEAD_DIM), lambda bi, h, k=k: (bi, 0, k * nh + h))

    def cseg(k):
        return pl.BlockSpec((1, lc, HEAD_DIM), lambda bi, h, k=k: (bi, 0, k * nh + h))

    return pl.pallas_call(
        _hgrn_kernel,
        grid=(b, nh),
        in_specs=[seg(0), seg(1), seg(2), seg(3), seg(4), cseg(0), cseg(1), cseg(2),
                  pl.BlockSpec((2, HEAD_DIM), lambda bi, h: (0, h)),
                  pl.BlockSpec((1, HEAD_DIM), lambda bi, h: (0, h))],
        out_specs=pl.BlockSpec((1, l, HEAD_DIM), lambda bi, h: (bi, 0, h)),
        out_shape=jax.ShapeDtypeStruct((b, l, width_a), bf16),
        scratch_shapes=[pltpu.VMEM((l, HEAD_DIM), f32), pltpu.VMEM((l, HEAD_DIM), f32),
                        pltpu.VMEM((2, l, HEAD_DIM), bf16), pltpu.VMEM((2, l, HEAD_DIM), bf16),
                        pltpu.VMEM((2, l, HEAD_DIM), bf16), pltpu.VMEM((2, l, HEAD_DIM), f32),
                        pltpu.VMEM((2, HEAD_DIM, HEAD_DIM), f32)],
        compiler_params=_cparams(2),
        name="hgrn",
    )(proj, proj, proj, proj, proj, proj_c, proj_c, proj_c, lb, norm_g.reshape(1, width_a))


def _residual_norm_route(x, y, gt, g2, sc, sh, wr_ref, br_ref, x1_ref, h2_ref, route_ref, cnt_ref, cnt_acc):
    @pl.when((pl.program_id(0) == 0) & (pl.program_id(1) == 0))
    def _():
        cnt_acc[...] = jnp.zeros_like(cnt_acc)

    x1 = x + gt * y
    x1_ref[0] = x1
    h2 = _rms_mod(x1, g2, sc, sh)
    h2_ref[0] = h2
    logits = jnp.dot(h2.astype(bf16), wr_ref[...], preferred_element_type=f32) + br_ref[...]
    tm = logits.shape[0]
    lane = lax.broadcasted_iota(i32, (tm, ROUTE_LANES), 1)
    lane_f = lane.astype(f32)
    neg = -jnp.inf
    gl = jnp.where(lane < N_EXPERT_GROUPS, logits, neg)
    gmax = jnp.max(gl, axis=-1, keepdims=True)
    p_top = 1.0 / jnp.sum(jnp.exp(gl - gmax), axis=-1, keepdims=True)
    grp = jnp.min(jnp.where(gl == gmax, lane_f, float(ROUTE_LANES)), axis=-1, keepdims=True)
    lo = float(N_EXPERT_GROUPS) + grp * float(EXPERTS_PER_GROUP)
    emask = (lane_f >= lo) & (lane_f < lo + float(EXPERTS_PER_GROUP))
    el = jnp.where(emask, logits, neg)
    emax = jnp.max(el, axis=-1, keepdims=True)
    esum = jnp.sum(jnp.exp(el - emax), axis=-1, keepdims=True)
    i1 = jnp.min(jnp.where(el == emax, lane_f, float(ROUTE_LANES)), axis=-1, keepdims=True)
    p1 = 1.0 / esum
    el2 = jnp.where(lane_f == i1, neg, el)
    emax2 = jnp.max(el2, axis=-1, keepdims=True)
    i2 = jnp.min(jnp.where(el2 == emax2, lane_f, float(ROUTE_LANES)), axis=-1, keepdims=True)
    p2 = jnp.exp(emax2 - emax) / esum
    den = p1 + p2
    w1 = p_top * p1 / den
    w2 = p_top * p2 / den
    e1 = i1 - float(N_EXPERT_GROUPS)
    e2 = i2 - float(N_EXPERT_GROUPS)
    hot1 = lane_f == i1
    hot2 = lane_f == i2
    both = jnp.where(hot1 | hot2, 1.0, 0.0)
    trow = lax.broadcasted_iota(i32, (tm, tm), 0)
    tcol = lax.broadcasted_iota(i32, (tm, tm), 1)
    before = (tcol < trow).astype(bf16)
    base = jnp.dot(before, both.astype(bf16), preferred_element_type=f32) + cnt_acc[...]
    r1 = jnp.sum(jnp.where(hot1, base, 0.0), axis=-1, keepdims=True)
    r2 = jnp.sum(jnp.where(hot2, base, 0.0), axis=-1, keepdims=True)
    cnt = cnt_acc[...] + jnp.sum(both, axis=0, keepdims=True)
    cnt_acc[...] = cnt
    cnt_ref[...] = cnt
    route = jnp.zeros((tm, ROUTE_LANES), f32)
    for k, val in enumerate((e1, e2, w1, w2, r1, r2)):
        route = jnp.where(lane == k, val, route)
    route_ref[0] = route


def _gelu_tanh(v):
    return 0.5 * v * (1.0 + jnp.tanh(0.7978845608028654 * (v + 0.044715 * (v * v * v))))


def _even_out_kernel(x_ref, a_ref, u_ref, v_ref, lng_ref, lnb_ref, ws_ref, bs_ref, wo_ref, gt_ref,
                     g2_ref, sc_ref, sh_ref, wr_ref, br_ref, x1_ref, h2_ref, route_ref, cnt_ref, bm_ref, cnt_acc):
    tm = x_ref.shape[1]
    wb = u_ref.shape[2]
    gd = wb // N_GROUPS_B
    gv = _gelu_tanh(v_ref[0].astype(f32))
    mu = jnp.mean(gv, axis=-1, keepdims=True)
    dv = gv - mu
    var = jnp.mean(dv * dv, axis=-1, keepdims=True)
    vn = (dv * lax.rsqrt(var + EPS) * lng_ref[...] + lnb_ref[...]).astype(bf16)
    for nb in range(tm // CHUNK_B):
        rs = slice(nb * CHUNK_B, (nb + 1) * CHUNK_B)
        for g in range(N_GROUPS_B):
            cs = slice(g * gd, (g + 1) * gd)
            s = jnp.dot(ws_ref[g], vn[rs, cs], preferred_element_type=f32) + bs_ref[:, g:g + 1]
            bm_ref[rs, cs] = (_gelu_tanh(u_ref[0, rs, cs].astype(f32)) * s).astype(bf16)
    wa = a_ref.shape[2]
    y = (jnp.dot(a_ref[0], wo_ref[0:wa, :], preferred_element_type=f32)
         + jnp.dot(bm_ref[...], wo_ref[wa:, :], preferred_element_type=f32))
    _residual_norm_route(x_ref[0], y, gt_ref[0], g2_ref[...], sc_ref[0], sh_ref[0], wr_ref, br_ref,
                         x1_ref, h2_ref, route_ref, cnt_ref, cnt_acc)


def _tail_specs(d, tm):
    vec = pl.BlockSpec((1, d), lambda bi, i: (0, 0))
    per_b = pl.BlockSpec((1, 1, d), lambda bi, i: (bi, 0, 0))
    in_specs = [per_b, vec, per_b, per_b,
                pl.BlockSpec((d, ROUTE_LANES), lambda bi, i: (0, 0)),
                pl.BlockSpec((1, ROUTE_LANES), lambda bi, i: (0, 0))]
    out_specs = [pl.BlockSpec((1, tm, d), lambda bi, i: (bi, i, 0)),
                 pl.BlockSpec((1, tm, d), lambda bi, i: (bi, i, 0)),
                 pl.BlockSpec((1, tm, ROUTE_LANES), lambda bi, i: (bi, i, 0)),
                 pl.BlockSpec((1, ROUTE_LANES), lambda bi, i: (0, 0))]
    return in_specs, out_specs


def _tail_out_shapes(b, l, d):
    return [jax.ShapeDtypeStruct((b, l, d), f32), jax.ShapeDtypeStruct((b, l, d), f32),
            jax.ShapeDtypeStruct((b, l, ROUTE_LANES), f32), jax.ShapeDtypeStruct((1, ROUTE_LANES), f32)]


def _even_out(x, a, proj, ln_g, ln_b, w_s, b_s_t, w_out, gt, g2, sc, sh, wr, br, tm):
    b, l, d = x.shape
    wa = a.shape[2]
    wb = ln_g.shape[0]
    ub = (5 * wa) // wb
    tail_in, tail_out = _tail_specs(d, tm)
    return pl.pallas_call(
        _even_out_kernel,
        grid=(b, l // tm),
        in_specs=[
            pl.BlockSpec((1, tm, d), lambda bi, i: (bi, i, 0)),
            pl.BlockSpec((1, tm, wa), lambda bi, i: (bi, i, 0)),
            pl.BlockSpec((1, tm, wb), lambda bi, i: (bi, i, ub)),
            pl.BlockSpec((1, tm, wb), lambda bi, i: (bi, i, ub + 1)),
            pl.BlockSpec((1, wb), lambda bi, i: (0, 0)),
            pl.BlockSpec((1, wb), lambda bi, i: (0, 0)),
            pl.BlockSpec((N_GROUPS_B, CHUNK_B, CHUNK_B), lambda bi, i: (0, 0, 0)),
            pl.BlockSpec((CHUNK_B, N_GROUPS_B), lambda bi, i: (0, 0)),
            pl.BlockSpec((wa + wb, d), lambda bi, i: (0, 0)),
        ] + tail_in,
        out_specs=tail_out,
        out_shape=_tail_out_shapes(b, l, d),
        scratch_shapes=[pltpu.VMEM((tm, wb), bf16), pltpu.VMEM((1, ROUTE_LANES), f32)],
        compiler_params=_cparams(2),
        name="even_out",
    )(x, a, proj, proj, ln_g.reshape(1, wb), ln_b.reshape(1, wb), w_s, b_s_t, w_out, gt, g2.reshape(1, d), sc, sh, wr, br)


def _odd_out_kernel(hx_ref, bg_ref, cg_ref, cw_ref, x_ref, wo_ref, gt_ref,
                    g2_ref, sc_ref, sh_ref, wr_ref, br_ref, x1_ref, h2_ref, route_ref, cnt_ref, cnt_acc):
    tm = x_ref.shape[1]
    z = cg_ref[0].astype(f32) * hx_ref[0].astype(f32)
    pos = lax.broadcasted_iota(i32, (tm, 1), 0) % GRID_W
    z_prev = jnp.where(pos == 0, 0.0, pltpu.roll(z, 1, 0))
    z_next = jnp.where(pos == GRID_W - 1, 0.0, pltpu.roll(z, tm - 1, 0))
    conv = z_prev * cw_ref[0:1, :] + z * cw_ref[1:2, :] + z_next * cw_ref[2:3, :]
    t = (bg_ref[0].astype(f32) * conv).astype(bf16)
    y = jnp.dot(t, wo_ref[...], preferred_element_type=f32)
    _residual_norm_route(x_ref[0], y, gt_ref[0], g2_ref[...], sc_ref[0], sh_ref[0], wr_ref, br_ref,
                         x1_ref, h2_ref, route_ref, cnt_ref, cnt_acc)


def _odd_out(x, pc, conv_w, w_out, gt, g2, sc, sh, wr, br, tm):
    b, l, d = x.shape
    wc = w_out.shape[0]
    tail_in, tail_out = _tail_specs(d, tm)
    return pl.pallas_call(
        _odd_out_kernel,
        grid=(b, l // tm),
        in_specs=[
            pl.BlockSpec((1, tm, wc), lambda bi, i: (bi, i, 0)),
            pl.BlockSpec((1, tm, wc), lambda bi, i: (bi, i, 1)),
            pl.BlockSpec((1, tm, wc), lambda bi, i: (bi, i, 2)),
            pl.BlockSpec((conv_w.shape[0], wc), lambda bi, i: (0, 0)),
            pl.BlockSpec((1, tm, d), lambda bi, i: (bi, i, 0)),
            pl.BlockSpec((wc, d), lambda bi, i: (0, 0)),
        ] + tail_in,
        out_specs=tail_out,
        out_shape=_tail_out_shapes(b, l, d),
        scratch_shapes=[pltpu.VMEM((1, ROUTE_LANES), f32)],
        compiler_params=_cparams(2),
        name="odd_out",
    )(pc, pc, pc, conv_w, x, w_out, gt, g2.reshape(1, d), sc, sh, wr, br)


def _row_gather_start(idx_ref, n_rows, src_hbm, dst_buf, sem):
    def body(r, carry):
        t = idx_ref[0, 0, r]
        pltpu.make_async_copy(src_hbm.at[pl.ds(t, 1), :], dst_buf.at[pl.ds(r, 1), :], sem).start()
        return carry

    lax.fori_loop(0, n_rows, body, 0, unroll=8)


def _row_gather_wait(n_rows, src_hbm, dst_buf, sem):
    def body(r, carry):
        pltpu.make_async_copy(src_hbm.at[pl.ds(0, 1), :], dst_buf.at[pl.ds(r, 1), :], sem).wait()
        return carry

    lax.fori_loop(0, n_rows, body, 0, unroll=8)


def _dispatch_kernel(pos_ref, h_ref, xs_hbm, sem):
    tm = h_ref.shape[0]

    def start(r, carry):
        for k in range(TOP_K):
            pltpu.make_async_copy(h_ref.at[pl.ds(r, 1), :], xs_hbm.at[pl.ds(pos_ref[0, 0, k * tm + r], 1), :],
                                  sem.at[0]).start()
        return carry

    lax.fori_loop(0, tm, start, 0, unroll=8)

    def wait(r, carry):
        pltpu.make_async_copy(h_ref.at[pl.ds(0, 1), :], xs_hbm.at[pl.ds(0, 1), :], sem.at[0]).wait()
        return carry

    lax.fori_loop(0, TOP_K * tm, wait, 0, unroll=8)


def _pos_blocks(pos, tm):
    nt = pos.shape[0] // tm
    return pos.reshape(nt, tm, TOP_K).transpose(0, 2, 1).reshape(nt, 1, TOP_K * tm)


def _moe_dispatch(h2, pos, tm):
    t, d = h2.shape
    return pl.pallas_call(
        _dispatch_kernel,
        grid=(t // tm,),
        in_specs=[pl.BlockSpec((1, 1, TOP_K * tm), lambda i: (i, 0, 0), memory_space=pltpu.SMEM),
                  pl.BlockSpec((tm, d), lambda i: (i, 0))],
        out_specs=pl.BlockSpec(memory_space=pl.ANY),
        out_shape=jax.ShapeDtypeStruct((t * TOP_K, d), h2.dtype),
        scratch_shapes=[pltpu.SemaphoreType.DMA((1,))],
        compiler_params=_cparams(1),
        name="moe_dispatch",
    )(_pos_blocks(pos, tm), h2)


def _ffn_kernel(blk_ref, e_ref, lo_ref, hi_ref, x_ref, wg_ref, wu_ref, wd_ref, y_ref, wg_s, wu_s, wd_s):
    w = pl.program_id(0)
    lo = lo_ref[w]
    hi = hi_ref[w]
    prev = jnp.maximum(w - 1, 0)

    @pl.when(hi > lo)
    def _():
        @pl.when((w == 0) | (e_ref[w] != e_ref[prev]))
        def _():
            wg_s[...] = wg_ref[0, 0].astype(bf16)
            wu_s[...] = wu_ref[0, 0].astype(bf16)
            wd_s[...] = wd_ref[0, 0].astype(bf16)

        xb = x_ref[...].astype(bf16)
        gate = jnp.dot(xb, wg_s[...], preferred_element_type=f32)
        up = jnp.dot(xb, wu_s[...], preferred_element_type=f32)
        mid = (_silu(gate) * up).astype(bf16)
        y = jnp.dot(mid, wd_s[...], preferred_element_type=f32)
        r = lax.broadcasted_iota(i32, (y.shape[0], 1), 0)
        mine = (r >= lo) & (r < hi)
        first_visit = (w == 0) | (blk_ref[w] != blk_ref[prev])

        @pl.when(first_visit)
        def _():
            y_ref[...] = jnp.where(mine, y, 0.0)

        @pl.when(jnp.logical_not(first_visit))
        def _():
            y_ref[...] = jnp.where(mine, y, y_ref[...])


def _moe_ffn(xs, items, layer, w_gate, w_up, w_down):
    rows, d = xs.shape
    bm = MOE_BLOCK
    ff = w_gate.shape[3]
    n_items = items[0].shape[0]
    grid_spec = pltpu.PrefetchScalarGridSpec(
        num_scalar_prefetch=4,
        grid=(n_items,),
        in_specs=[
            pl.BlockSpec((bm, d), lambda w, blk, e, lo, hi: (blk[w], 0)),
            pl.BlockSpec((1, 1, d, ff), lambda w, blk, e, lo, hi: (layer, e[w], 0, 0)),
            pl.BlockSpec((1, 1, d, ff), lambda w, blk, e, lo, hi: (layer, e[w], 0, 0)),
            pl.BlockSpec((1, 1, ff, d), lambda w, blk, e, lo, hi: (layer, e[w], 0, 0)),
        ],
        out_specs=pl.BlockSpec((bm, d), lambda w, blk, e, lo, hi: (blk[w], 0)),
        scratch_shapes=[pltpu.VMEM((d, ff), bf16), pltpu.VMEM((d, ff), bf16), pltpu.VMEM((ff, d), bf16)],
    )
    return pl.pallas_call(
        _ffn_kernel,
        grid_spec=grid_spec,
        out_shape=jax.ShapeDtypeStruct((rows, d), f32),
        compiler_params=_cparams(1),
        name="moe_ffn",
    )(*items, xs, w_gate, w_up, w_down)


def _combine_kernel(pos_cur_ref, pos_nxt_ref, x_ref, gt_ref, ng_ref, nsc_ref, nsh_ref, route_ref, ys_hbm, *rest, final):
    out_refs, (ybuf, sem) = rest[:-2], rest[-2:]
    i = pl.program_id(0)
    n = pl.num_programs(0)
    tm = x_ref.shape[1]
    slot = lax.rem(i, 2)

    @pl.when(i == 0)
    def _():
        _row_gather_start(pos_cur_ref, TOP_K * tm, ys_hbm, ybuf.at[0], sem.at[0])

    @pl.when(i + 1 < n)
    def _():
        _row_gather_start(pos_nxt_ref, TOP_K * tm, ys_hbm, ybuf.at[1 - slot], sem.at[1 - slot])

    _row_gather_wait(TOP_K * tm, ys_hbm, ybuf.at[slot], sem.at[slot])
    route = route_ref[0]
    f = ybuf[slot, 0:tm, :] * route[:, TOP_K:TOP_K + 1] + ybuf[slot, tm:2 * tm, :] * route[:, TOP_K + 1:TOP_K + 2]
    xo = x_ref[0] + gt_ref[0] * f
    hn = _rms_mod(xo, ng_ref[...], nsc_ref[0], nsh_ref[0])
    if final:
        out_refs[0][0] = hn
    else:
        out_refs[0][0] = xo
        out_refs[1][0] = hn.astype(out_refs[1].dtype)


def _moe_combine(x1, gt, norm_g, norm_sc, norm_sh, route, ys, pos, tm, final):
    b, l, d = x1.shape
    nt = (b * l) // tm
    per_b = l // tm
    pos3 = _pos_blocks(pos, tm)
    tile = pl.BlockSpec((1, tm, d), lambda i: (i // per_b, i % per_b, 0))
    per_b_vec = pl.BlockSpec((1, 1, d), lambda i: (i // per_b, 0, 0))
    if final:
        out_specs, out_shape = tile, jax.ShapeDtypeStruct((b, l, d), f32)
    else:
        out_specs = [tile, tile]
        out_shape = [jax.ShapeDtypeStruct((b, l, d), f32), jax.ShapeDtypeStruct((b, l, d), bf16)]
    return pl.pallas_call(
        functools.partial(_combine_kernel, final=final),
        grid=(nt,),
        in_specs=[
            pl.BlockSpec((1, 1, TOP_K * tm), lambda i: (i, 0, 0), memory_space=pltpu.SMEM),
            pl.BlockSpec((1, 1, TOP_K * tm), lambda i: (jnp.minimum(i + 1, nt - 1), 0, 0), memory_space=pltpu.SMEM),
            tile, per_b_vec,
            pl.BlockSpec((1, d), lambda i: (0, 0)),
            per_b_vec, per_b_vec,
            pl.BlockSpec((1, tm, ROUTE_LANES), lambda i: (i // per_b, i % per_b, 0)),
            pl.BlockSpec(memory_space=pl.ANY),
        ],
        out_specs=out_specs,
        out_shape=out_shape,
        scratch_shapes=[pltpu.VMEM((2, TOP_K * tm, d), f32), pltpu.SemaphoreType.DMA((2,))],
        compiler_params=_cparams(1),
        name="moe_combine",
    )(pos3, pos3, x1, gt, norm_g.reshape(1, d), norm_sc, norm_sh, route, ys)


def _dispatch_plan(route, cnt, n_tok):
    bm = MOE_BLOCK
    n_rows = n_tok * TOP_K
    n_blk = n_rows // bm
    experts = jnp.arange(N_EXPERTS, dtype=i32)
    counts = cnt[0, N_EXPERT_GROUPS:N_EXPERT_GROUPS + N_EXPERTS].astype(i32)
    ends = jnp.cumsum(counts)
    starts = ends - counts
    eid = route[..., 0:TOP_K].astype(i32).reshape(n_tok, TOP_K)
    rank = route[..., 2 * TOP_K:3 * TOP_K].astype(i32).reshape(n_tok, TOP_K)
    pos = rank + jnp.sum(jnp.where(eid[..., None] == experts, starts, 0), axis=-1)
    first_blk = starts // bm
    n_it = jnp.where(counts > 0, (ends - 1) // bm - first_blk + 1, 0)
    it_end = jnp.cumsum(n_it)
    it_start = it_end - n_it
    w = jnp.arange(n_blk + N_EXPERTS, dtype=i32)
    live = w < it_end[-1]
    e_last = jnp.max(jnp.where(counts > 0, experts, 0))
    e_w = jnp.where(live, jnp.sum((it_end[None, :] <= w[:, None]).astype(i32), axis=1), e_last)
    hot = e_w[:, None] == experts
    pick = lambda tab: jnp.sum(jnp.where(hot, tab, 0), axis=1)
    blk_w = jnp.where(live, pick(first_blk) + w - pick(it_start), n_blk - 1)
    lo_w = jnp.where(live, jnp.maximum(pick(starts), blk_w * bm) - blk_w * bm, 0)
    hi_w = jnp.where(live, jnp.minimum(pick(ends), (blk_w + 1) * bm) - blk_w * bm, 0)
    return pos, (blk_w, e_w, lo_w, hi_w)


def _hier_moe(x1, h2, route, cnt, gt, norm_g, norm_sc, norm_sh, layer, w_gate, w_up, w_down, final):
    b, l, d = x1.shape
    n_tok = b * l
    assert (n_tok * TOP_K) % MOE_BLOCK == 0
    pos, items = _dispatch_plan(route, cnt, n_tok)
    xs = _moe_dispatch(h2.reshape(n_tok, d), pos, 512)
    ys = _moe_ffn(xs, items, layer, w_gate, w_up, w_down)
    return _moe_combine(x1, gt, norm_g, norm_sc, norm_sh, route, ys, pos, 256, final)


def _router_params(rg_w, rg_b, re_w, re_b):
    d = rg_w.shape[0]
    pad = ROUTE_LANES - N_EXPERT_GROUPS - N_EXPERTS
    wr = jnp.concatenate([rg_w, re_w, jnp.zeros((d, pad), f32)], axis=1).astype(bf16)
    br = jnp.concatenate([rg_b, re_b, jnp.zeros((pad,), f32)]).reshape(1, ROUTE_LANES)
    return wr, br


def kernel(x, c, ctx, c_ctx, w_mod, b_mod, norm_mix_g, norm_ffn_g, final_g, even_w_in, even_w_out, hgrn_lb_raw,
           hgrn_norm_g, gmlp_ln_g, gmlp_ln_b, gmlp_w_s, gmlp_b_s, odd_w_in, odd_conv_w, odd_w_out, router_g_w,
           router_g_b, router_e_w, router_e_b, exp_w_gate, exp_w_up, exp_w_down):
    b, l, d = x.shape
    depth = w_mod.shape[0]
    assert depth == 2, "layer plan below is written for one even and one odd layer"
    width_a = hgrn_norm_g.shape[1]
    lc = ctx.shape[1]

    rows = -(-(b + 1) // 8) * 8
    c_all = jnp.zeros((rows, d), f32).at[:b].set(c).at[b].set(c_ctx)
    mod = _mod(c_all, w_mod, b_mod)

    def latent_mod(layer):
        return [m.reshape(b, 1, d) for m in jnp.split(mod[layer, :b], 6, axis=-1)]

    sh_m, sc_m, gt_m, sh_f, sc_f, gt_f = latent_mod(0)
    csh_m = jnp.broadcast_to(mod[0, b, 0:d].reshape(1, 1, d), (b, 1, d))
    csc_m = jnp.broadcast_to(mod[0, b, d:2 * d].reshape(1, 1, d), (b, 1, d))
    w_in = even_w_in[0].astype(bf16)
    proj = _norm_matmul(x, norm_mix_g[0], sc_m, sh_m, w_in, min(l, 1024), 1024)
    proj_c = _norm_matmul(ctx, norm_mix_g[0], csc_m, csh_m, w_in[:, width_a:4 * width_a], lc, 1024)
    lb = jnp.cumsum(jax.nn.softmax(hgrn_lb_raw.astype(f32), axis=0), axis=0)[0]
    a = _hgrn(proj, proj_c, lb, hgrn_norm_g[0], width_a)
    wr, br = _router_params(router_g_w[0], router_g_b[0], router_e_w[0], router_e_b[0])
    x1, h2, route, cnt = _even_out(x, a, proj, gmlp_ln_g[0], gmlp_ln_b[0], gmlp_w_s[0].astype(bf16), gmlp_b_s[0].T,
                                   even_w_out[0].astype(bf16), gt_m, norm_ffn_g[0], sc_f, sh_f, wr, br, 512)
    sh_m1, sc_m1, gt_m1, sh_f1, sc_f1, gt_f1 = latent_mod(1)
    x, h = _hier_moe(x1, h2, route, cnt, gt_f, norm_mix_g[1], sc_m1, sh_m1, 0, exp_w_gate, exp_w_up, exp_w_down, False)

    pc = _matmul(h.reshape(b * l, d), odd_w_in[0].astype(bf16), 2048, 1024).reshape(b, l, -1)
    wr, br = _router_params(router_g_w[1], router_g_b[1], router_e_w[1], router_e_b[1])
    x1, h2, route, cnt = _odd_out(x, pc, odd_conv_w[0], odd_w_out[0].astype(bf16), gt_m1, norm_ffn_g[1], sc_f1, sh_f1,
                                  wr, br, 512)
    zero = jnp.zeros((b, 1, d), f32)
    return _hier_moe(x1, h2, route, cnt, gt_f1, final_g, zero, zero, 1, exp_w_gate, exp_w_up, exp_w_down, True)
```

```python
import functools

import jax
import jax.numpy as jnp
from jax import lax
from jax.experimental import pallas as pl
from jax.experimental.pallas import tpu as pltpu

f32 = jnp.float32
bf16 = jnp.bfloat16
i32 = jnp.int32

EPS = 1e-6
HEAD_DIM = 128
SCAN_CHUNK = 64
GATE_ROWS = 128
CHUNK_B = 128
N_GROUPS_B = 8
GRID_W = 64
N_EXPERT_GROUPS = 4
EXPERTS_PER_GROUP = 8
N_EXPERTS = N_EXPERT_GROUPS * EXPERTS_PER_GROUP
TOP_K = 2
ROUTE_LANES = 128
MOE_BLOCK = 256
SUB = 256
VMEM_LIMIT = 56 * 1024 * 1024

NT = (((1,), (1,)), ((), ()))
TN = (((0,), (0,)), ((), ()))


def _cparams(n_axes):
    return pltpu.CompilerParams(dimension_semantics=("arbitrary",) * n_axes, vmem_limit_bytes=VMEM_LIMIT)


def _silu(v):
    return v * jax.nn.sigmoid(v)


def _mod_kernel(c_ref, w_ref, b_ref, o_ref):
    s = _silu(c_ref[...])
    o_ref[0] = jnp.dot(s, w_ref[0], preferred_element_type=f32, precision=lax.Precision.HIGHEST) + b_ref[0]


def _mod(c_all, w_mod, b_mod):
    depth, d, n = w_mod.shape
    rows = c_all.shape[0]
    tn = 1024
    return pl.pallas_call(
        _mod_kernel,
        grid=(depth, n // tn),
        in_specs=[
            pl.BlockSpec((rows, d), lambda l, j: (0, 0)),
            pl.BlockSpec((1, d, tn), lambda l, j: (l, 0, j)),
            pl.BlockSpec((1, 1, tn), lambda l, j: (l, 0, j)),
        ],
        out_specs=pl.BlockSpec((1, rows, tn), lambda l, j: (l, 0, j)),
        out_shape=jax.ShapeDtypeStruct((depth, rows, n), f32),
        compiler_params=_cparams(2),
        name="mod",
    )(c_all, w_mod, b_mod.reshape(depth, 1, n))


def _rms_mod(x, g, sc, sh):
    y = x * lax.rsqrt(jnp.mean(x * x, axis=-1, keepdims=True) + EPS)
    return (y * g) * (1.0 + sc) + sh


def _norm_matmul_kernel(x_ref, g_ref, sc_ref, sh_ref, w_ref, o_ref, h_ref):
    @pl.when(pl.program_id(2) == 0)
    def _():
        h_ref[...] = _rms_mod(x_ref[0], g_ref[...], sc_ref[0], sh_ref[0]).astype(bf16)

    o_ref[0] = jnp.dot(h_ref[...], w_ref[...], preferred_element_type=f32).astype(o_ref.dtype)


def _norm_matmul(x, g, sc, sh, w, tm, tn):
    b, l, d = x.shape
    n = w.shape[1]
    return pl.pallas_call(
        _norm_matmul_kernel,
        grid=(b, l // tm, n // tn),
        in_specs=[
            pl.BlockSpec((1, tm, d), lambda bi, i, j: (bi, i, 0)),
            pl.BlockSpec((1, d), lambda bi, i, j: (0, 0)),
            pl.BlockSpec((1, 1, d), lambda bi, i, j: (bi, 0, 0)),
            pl.BlockSpec((1, 1, d), lambda bi, i, j: (bi, 0, 0)),
            pl.BlockSpec((d, tn), lambda bi, i, j: (0, j)),
        ],
        out_specs=pl.BlockSpec((1, tm, tn), lambda bi, i, j: (bi, i, j)),
        out_shape=jax.ShapeDtypeStruct((b, l, n), bf16),
        scratch_shapes=[pltpu.VMEM((tm, d), bf16)],
        compiler_params=_cparams(3),
        name="norm_matmul",
    )(x, g.reshape(1, d), sc, sh, w)


def _matmul_kernel(x_ref, w_ref, o_ref):
    o_ref[...] = jnp.dot(x_ref[...], w_ref[...], preferred_element_type=f32).astype(o_ref.dtype)


def _matmul(x, w, tm, tn):
    m, d = x.shape
    n = w.shape[1]
    tm = min(tm, m)
    return pl.pallas_call(
        _matmul_kernel,
        grid=(m // tm, n // tn),
        in_specs=[pl.BlockSpec((tm, d), lambda i, j: (i, 0)), pl.BlockSpec((d, tn), lambda i, j: (0, j))],
        out_specs=pl.BlockSpec((tm, tn), lambda i, j: (i, j)),
        out_shape=jax.ShapeDtypeStruct((m, n), bf16),
        compiler_params=_cparams(2),
        name="matmul",
    )(x, w)


def _hgrn_kernel(q_ref, ff_ref, fb_ref, v_ref, g_ref, cff_ref, cfb_ref, cv_ref, lb_ref, ng_ref,
                 a_ref, of_ref, ob_ref, qd_ref, kd_ref, ks_ref, dec_ref, st_ref):
    C = SCAN_CHUNK
    G = GATE_ROWS
    L = q_ref.shape[1]
    Lc = cff_ref.shape[1]
    n = L // C
    row = lax.broadcasted_iota(i32, (C, C), 0)
    col = lax.broadcasted_iota(i32, (C, C), 1)
    lower = col <= row
    upper = col >= row
    grow = lax.broadcasted_iota(i32, (G, G), 0)
    gcol = lax.broadcasted_iota(i32, (G, G), 1)
    same_chunk = (grow // C) == (gcol // C)
    tri = ((same_chunk & (gcol <= grow)).astype(bf16), (same_chunk & (gcol >= grow)).astype(bf16))
    lbs = (lb_ref[0:1, :], lb_ref[1:2, :])

    def gate_block(f_raw, d):
        f = lbs[d] + (1.0 - lbs[d]) * jax.nn.sigmoid(f_raw.astype(f32))
        logf = jnp.log(f)
        hi = logf.astype(bf16)
        low = (logf - hi.astype(f32)).astype(bf16)
        cum = jnp.dot(tri[d], hi, preferred_element_type=f32) + jnp.dot(tri[d], low, preferred_element_type=f32)
        k3 = (1.0 - f).reshape(G // C, C, HEAD_DIM)
        cum3 = cum.reshape(G // C, C, HEAD_DIM)
        edge3 = cum3[:, C - 1:C, :] if d == 0 else cum3[:, 0:1, :]
        k_dec = (k3 * jnp.exp(-cum3)).reshape(G, HEAD_DIM).astype(bf16)
        k_st = (k3 * jnp.exp(edge3 - cum3)).reshape(G, HEAD_DIM).astype(bf16)
        dec = [jnp.exp(edge3[c]) for c in range(G // C)]
        return cum, k_dec, k_st, dec

    f_refs = (cff_ref, cfb_ref)
    for d in range(2):
        st = jnp.zeros((HEAD_DIM, HEAD_DIM), f32)
        blocks = range(Lc // G) if d == 0 else range(Lc // G - 1, -1, -1)
        for gb in blocks:
            _, _, k_st, dec = gate_block(f_refs[d][0, gb * G:(gb + 1) * G, :], d)
            chunks = range(G // C) if d == 0 else range(G // C - 1, -1, -1)
            for c in chunks:
                v = cv_ref[0, gb * G + c * C:gb * G + (c + 1) * C, :]
                st = (dec[c] * st
                      + lax.dot_general(v, k_st[c * C:(c + 1) * C, :], TN, preferred_element_type=f32))
        st_ref[d] = st

    def precompute(i, carry):
        off = pl.multiple_of(i * G, G)
        rows = pl.ds(off, G)
        qs = _silu(q_ref[0, rows, :].astype(f32))
        for d, f_ref in enumerate((ff_ref, fb_ref)):
            cum, k_dec, k_st, dec = gate_block(f_ref[0, rows, :], d)
            qd_ref[d, rows, :] = (qs * jnp.exp(cum)).astype(bf16)
            kd_ref[d, rows, :] = k_dec
            ks_ref[d, rows, :] = k_st
            for c in range(G // C):
                dec_ref[d, pl.ds(off + c * C, 1), :] = dec[c]
        return carry

    lax.fori_loop(0, L // G, precompute, 0)

    def chunk_step(off, d, mask, o_ref):
        rows = pl.ds(off, C)
        qd = qd_ref[d, rows, :]
        v = v_ref[0, rows, :]
        s = lax.dot_general(qd, kd_ref[d, rows, :], NT, preferred_element_type=f32)
        s = jnp.where(mask, s, 0.0).astype(bf16)
        st = st_ref[d]
        o_ref[rows, :] = (jnp.dot(s, v, preferred_element_type=f32)
                          + lax.dot_general(qd, st.astype(bf16), NT, preferred_element_type=f32))
        st_ref[d] = (dec_ref[d, pl.ds(off, 1), :] * st
                     + lax.dot_general(v, ks_ref[d, rows, :], TN, preferred_element_type=f32))

    def body(c, carry):
        chunk_step(pl.multiple_of(c * C, C), 0, lower, of_ref)
        chunk_step(pl.multiple_of((n - 1 - c) * C, C), 1, upper, ob_ref)
        return carry

    lax.fori_loop(0, n, body, 0, unroll=4)

    R = 512 if L % 512 == 0 else C

    def epilogue(r, carry):
        rows = pl.ds(pl.multiple_of(r * R, R), R)
        o = of_ref[rows, :] + ob_ref[rows, :]
        o = o * lax.rsqrt(jnp.mean(o * o, axis=-1, keepdims=True) + EPS)
        o = o * ng_ref[...]
        a_ref[0, rows, :] = (o * _silu(g_ref[0, rows, :].astype(f32))).astype(a_ref.dtype)
        return carry

    lax.fori_loop(0, L // R, epilogue, 0)


def _hgrn(proj, proj_c, lb, norm_g, width_a):
    b, l, _ = proj.shape
    lc = proj_c.shape[1]
    nh = width_a // HEAD_DIM

    def seg(k):
        return pl.BlockSpec((1, l, HEAD_DIM), lambda bi, h, k=k: (bi, 0, k * nh + h))

    def cseg(k):
        return pl.BlockSpec((1, lc, HEAD_DIM), lambda bi, h, k=k: (bi, 0, k * nh + h))

    return pl.pallas_call(
        _hgrn_kernel,
        grid=(b, nh),
        in_specs=[seg(0), seg(1), seg(2), seg(3), seg(4), cseg(0), cseg(1), cseg(2),
                  pl.BlockSpec((2, HEAD_DIM), lambda bi, h: (0, h)),
                  pl.BlockSpec((1, HEAD_DIM), lambda bi, h: (0, h))],
        out_specs=pl.BlockSpec((1, l, HEAD_DIM), lambda bi, h: (bi, 0, h)),
        out_shape=jax.ShapeDtypeStruct((b, l, width_a), bf16),
        scratch_shapes=[pltpu.VMEM((l, HEAD_DIM), f32), pltpu.VMEM((l, HEAD_DIM), f32),
                        pltpu.VMEM((2, l, HEAD_DIM), bf16), pltpu.VMEM((2, l, HEAD_DIM), bf16),
                        pltpu.VMEM((2, l, HEAD_DIM), bf16), pltpu.VMEM((2, l, HEAD_DIM), f32),
                        pltpu.VMEM((2, HEAD_DIM, HEAD_DIM), f32)],
        compiler_params=_cparams(2),
        name="hgrn",
    )(proj, proj, proj, proj, proj, proj_c, proj_c, proj_c, lb, norm_g.reshape(1, width_a))


def _residual_norm_route(rows, valid, x, y, gt, g2, sc, sh, wr_ref, br_ref, x1_ref, h2_ref, route_ref, cnt_ref,
                         cnt_acc):
    x1 = x + gt * y
    x1_ref[0, rows, :] = x1
    h2 = _rms_mod(x1, g2, sc, sh)
    h2_ref[0, rows, :] = h2
    logits = jnp.dot(h2.astype(bf16), wr_ref[...], preferred_element_type=f32) + br_ref[...]
    tm = logits.shape[0]
    lane = lax.broadcasted_iota(i32, (tm, ROUTE_LANES), 1)
    lane_f = lane.astype(f32)
    neg = -jnp.inf
    gl = jnp.where(lane < N_EXPERT_GROUPS, logits, neg)
    gmax = jnp.max(gl, axis=-1, keepdims=True)
    p_top = 1.0 / jnp.sum(jnp.exp(gl - gmax), axis=-1, keepdims=True)
    grp = jnp.min(jnp.where(gl == gmax, lane_f, float(ROUTE_LANES)), axis=-1, keepdims=True)
    lo = float(N_EXPERT_GROUPS) + grp * float(EXPERTS_PER_GROUP)
    emask = (lane_f >= lo) & (lane_f < lo + float(EXPERTS_PER_GROUP))
    el = jnp.where(emask, logits, neg)
    emax = jnp.max(el, axis=-1, keepdims=True)
    esum = jnp.sum(jnp.exp(el - emax), axis=-1, keepdims=True)
    i1 = jnp.min(jnp.where(el == emax, lane_f, float(ROUTE_LANES)), axis=-1, keepdims=True)
    p1 = 1.0 / esum
    el2 = jnp.where(lane_f == i1, neg, el)
    emax2 = jnp.max(el2, axis=-1, keepdims=True)
    i2 = jnp.min(jnp.where(el2 == emax2, lane_f, float(ROUTE_LANES)), axis=-1, keepdims=True)
    p2 = jnp.exp(emax2 - emax) / esum
    den = p1 + p2
    w1 = p_top * p1 / den
    w2 = p_top * p2 / den
    e1 = i1 - float(N_EXPERT_GROUPS)
    e2 = i2 - float(N_EXPERT_GROUPS)
    hot1 = lane_f == i1
    hot2 = lane_f == i2
    both = jnp.where((hot1 | hot2) & valid, 1.0, 0.0)
    trow = lax.broadcasted_iota(i32, (tm, tm), 0)
    tcol = lax.broadcasted_iota(i32, (tm, tm), 1)
    before = (tcol < trow).astype(bf16)
    base = jnp.dot(before, both.astype(bf16), preferred_element_type=f32) + cnt_acc[...]
    r1 = jnp.sum(jnp.where(hot1, base, 0.0), axis=-1, keepdims=True)
    r2 = jnp.sum(jnp.where(hot2, base, 0.0), axis=-1, keepdims=True)
    cnt = cnt_acc[...] + jnp.sum(both, axis=0, keepdims=True)
    cnt_acc[...] = cnt
    cnt_ref[...] = cnt
    route = jnp.zeros((tm, ROUTE_LANES), f32)
    for k, val in enumerate((e1, e2, w1, w2, r1, r2)):
        route = jnp.where(lane == k, val, route)
    route_ref[0, rows, :] = route


def _out_pipeline(stage_fn, x_ref, wo_ref, gt_ref, g2_ref, sc_ref, sh_ref, wr_ref, br_ref,
                  x1_ref, h2_ref, route_ref, cnt_ref, t_bufs, y_bufs, cnt_acc):
    p = pl.program_id(0)

    @pl.when(p == 0)
    def _():
        for ref in (*t_bufs, *y_bufs, cnt_acc):
            ref[...] = jnp.zeros_like(ref)

    for half in range(2):
        rows = slice(half * SUB, (half + 1) * SUB)
        _residual_norm_route(rows, p > 0, x_ref[0, rows, :], y_bufs[half][...], gt_ref[0], g2_ref[...], sc_ref[0],
                             sh_ref[0], wr_ref, br_ref, x1_ref, h2_ref, route_ref, cnt_ref, cnt_acc)
        y_bufs[1 - half][...] = jnp.dot(t_bufs[1 - half][...], wo_ref[...], preferred_element_type=f32)
        stage_fn(rows, t_bufs[half])


def _gelu_tanh(v):
    return 0.5 * v * (1.0 + jnp.tanh(0.7978845608028654 * (v + 0.044715 * (v * v * v))))


def _even_out_kernel(a_ref, u_ref, v_ref, lng_ref, lnb_ref, ws_ref, bs_ref, x_ref, wo_ref, gt_ref, g2_ref, sc_ref,
                     sh_ref, wr_ref, br_ref, x1_ref, h2_ref, route_ref, cnt_ref, t0, t1, y0, y1, cnt_acc):
    wa = a_ref.shape[2]
    gd = u_ref.shape[2] // N_GROUPS_B

    def stage(rows, t_ref):
        t_ref[:, 0:wa] = a_ref[0, rows, :]
        gv = _gelu_tanh(v_ref[0, rows, :].astype(f32))
        mu = jnp.mean(gv, axis=-1, keepdims=True)
        dv = gv - mu
        var = jnp.mean(dv * dv, axis=-1, keepdims=True)
        vn = (dv * lax.rsqrt(var + EPS) * lng_ref[...] + lnb_ref[...]).astype(bf16)
        for nb in range(SUB // CHUNK_B):
            rs = slice(nb * CHUNK_B, (nb + 1) * CHUNK_B)
            ru = slice(rows.start + nb * CHUNK_B, rows.start + (nb + 1) * CHUNK_B)
            for g in range(N_GROUPS_B):
                cs = slice(g * gd, (g + 1) * gd)
                s = jnp.dot(ws_ref[g], vn[rs, cs], preferred_element_type=f32) + bs_ref[:, g:g + 1]
                t_ref[rs, wa + g * gd:wa + (g + 1) * gd] = (_gelu_tanh(u_ref[0, ru, cs].astype(f32)) * s).astype(bf16)

    _out_pipeline(stage, x_ref, wo_ref, gt_ref, g2_ref, sc_ref, sh_ref, wr_ref, br_ref,
                  x1_ref, h2_ref, route_ref, cnt_ref, (t0, t1), (y0, y1), cnt_acc)


def _odd_out_kernel(hx_ref, bg_ref, cg_ref, cw_ref, x_ref, wo_ref, gt_ref, g2_ref, sc_ref,
                    sh_ref, wr_ref, br_ref, x1_ref, h2_ref, route_ref, cnt_ref, t0, t1, y0, y1, cnt_acc):
    def stage(rows, t_ref):
        z = cg_ref[0, rows, :].astype(f32) * hx_ref[0, rows, :].astype(f32)
        pos = lax.broadcasted_iota(i32, (SUB, 1), 0) % GRID_W
        z_prev = jnp.where(pos == 0, 0.0, pltpu.roll(z, 1, 0))
        z_next = jnp.where(pos == GRID_W - 1, 0.0, pltpu.roll(z, SUB - 1, 0))
        conv = z_prev * cw_ref[0:1, :] + z * cw_ref[1:2, :] + z_next * cw_ref[2:3, :]
        t_ref[...] = (bg_ref[0, rows, :].astype(f32) * conv).astype(bf16)

    _out_pipeline(stage, x_ref, wo_ref, gt_ref, g2_ref, sc_ref, sh_ref, wr_ref, br_ref,
                  x1_ref, h2_ref, route_ref, cnt_ref, (t0, t1), (y0, y1), cnt_acc)


def _mixer_out(kernel_fn, name, stage_args, stage_specs, x, w_out, gt, g2, sc, sh, wr, br):
    b, l, d = x.shape
    blk = 2 * SUB
    per_b = l // blk
    n = b * per_b
    k = w_out.shape[0]

    def ahead(p):
        q = jnp.minimum(p, n - 1)
        return q // per_b, q % per_b

    def behind(p):
        q = jnp.maximum(p - 1, 0)
        return q // per_b, q % per_b

    const = lambda shape: pl.BlockSpec(shape, lambda p: (0,) * len(shape))
    per_b_vec = pl.BlockSpec((1, 1, d), lambda p: (behind(p)[0], 0, 0))
    tile = lambda width: pl.BlockSpec((1, blk, width), lambda p: (*behind(p), 0))
    return pl.pallas_call(
        kernel_fn,
        grid=(n + 1,),
        in_specs=stage_specs(ahead, blk, const) + [
            tile(d), const((k, d)), per_b_vec, const((1, d)), per_b_vec, per_b_vec,
            const((d, ROUTE_LANES)), const((1, ROUTE_LANES))],
        out_specs=[tile(d), tile(d), tile(ROUTE_LANES), const((1, ROUTE_LANES))],
        out_shape=[jax.ShapeDtypeStruct((b, l, d), f32), jax.ShapeDtypeStruct((b, l, d), f32),
                   jax.ShapeDtypeStruct((b, l, ROUTE_LANES), f32), jax.ShapeDtypeStruct((1, ROUTE_LANES), f32)],
        scratch_shapes=[pltpu.VMEM((SUB, k), bf16), pltpu.VMEM((SUB, k), bf16),
                        pltpu.VMEM((SUB, d), f32), pltpu.VMEM((SUB, d), f32), pltpu.VMEM((1, ROUTE_LANES), f32)],
        compiler_params=_cparams(1),
        name=name,
    )(*stage_args, x, w_out, gt, g2.reshape(1, d), sc, sh, wr, br)


def _even_out(x, a, proj, ln_g, ln_b, w_s, b_s_t, w_out, gt, g2, sc, sh, wr, br):
    wa = a.shape[2]
    wb = ln_g.shape[0]
    ub = (5 * wa) // wb

    def stage_specs(ahead, blk, const):
        return [pl.BlockSpec((1, blk, wa), lambda p: (*ahead(p), 0)),
                pl.BlockSpec((1, blk, wb), lambda p: (*ahead(p), ub)),
                pl.BlockSpec((1, blk, wb), lambda p: (*ahead(p), ub + 1)),
                const((1, wb)), const((1, wb)), const((N_GROUPS_B, CHUNK_B, CHUNK_B)), const((CHUNK_B, N_GROUPS_B))]

    return _mixer_out(_even_out_kernel, "even_out",
                      (a, proj, proj, ln_g.reshape(1, wb), ln_b.reshape(1, wb), w_s, b_s_t), stage_specs,
                      x, w_out, gt, g2, sc, sh, wr, br)


def _odd_out(x, pc, conv_w, w_out, gt, g2, sc, sh, wr, br):
    wc = w_out.shape[0]

    def stage_specs(ahead, blk, const):
        return [pl.BlockSpec((1, blk, wc), lambda p, j=j: (*ahead(p), j)) for j in range(3)] + [const(conv_w.shape)]

    return _mixer_out(_odd_out_kernel, "odd_out", (pc, pc, pc, conv_w), stage_specs,
                      x, w_out, gt, g2, sc, sh, wr, br)


def _row_gather_start(idx_ref, n_rows, src_hbm, dst_buf, sem):
    def body(r, carry):
        t = idx_ref[0, 0, r]
        pltpu.make_async_copy(src_hbm.at[pl.ds(t, 1), :], dst_buf.at[pl.ds(r, 1), :], sem).start()
        return carry

    lax.fori_loop(0, n_rows, body, 0, unroll=8)


def _row_gather_wait(n_rows, src_hbm, dst_buf, sem):
    def body(r, carry):
        pltpu.make_async_copy(src_hbm.at[pl.ds(0, 1), :], dst_buf.at[pl.ds(r, 1), :], sem).wait()
        return carry

    lax.fori_loop(0, n_rows, body, 0, unroll=8)


def _dispatch_kernel(pos_ref, h_ref, xs_hbm, sem):
    tm = h_ref.shape[0]

    def start(r, carry):
        for k in range(TOP_K):
            pltpu.make_async_copy(h_ref.at[pl.ds(r, 1), :], xs_hbm.at[pl.ds(pos_ref[0, 0, k * tm + r], 1), :],
                                  sem.at[0]).start()
        return carry

    lax.fori_loop(0, tm, start, 0, unroll=8)

    def wait(r, carry):
        pltpu.make_async_copy(h_ref.at[pl.ds(0, 1), :], xs_hbm.at[pl.ds(0, 1), :], sem.at[0]).wait()
        return carry

    lax.fori_loop(0, TOP_K * tm, wait, 0, unroll=8)


def _pos_blocks(pos, tm):
    nt = pos.shape[0] // tm
    return pos.reshape(nt, tm, TOP_K).transpose(0, 2, 1).reshape(nt, 1, TOP_K * tm)


def _moe_dispatch(h2, pos, tm):
    t, d = h2.shape
    return pl.pallas_call(
        _dispatch_kernel,
        grid=(t // tm,),
        in_specs=[pl.BlockSpec((1, 1, TOP_K * tm), lambda i: (i, 0, 0), memory_space=pltpu.SMEM),
                  pl.BlockSpec((tm, d), lambda i: (i, 0))],
        out_specs=pl.BlockSpec(memory_space=pl.ANY),
        out_shape=jax.ShapeDtypeStruct((t * TOP_K, d), h2.dtype),
        scratch_shapes=[pltpu.SemaphoreType.DMA((1,))],
        compiler_params=_cparams(1),
        name="moe_dispatch",
    )(_pos_blocks(pos, tm), h2)


def _ffn_kernel(blk_ref, e_ref, lo_ref, hi_ref, first_ref, nxt_ref, x_ref, wg_hbm, wu_hbm, wd_hbm, y_ref,
                wg_f, wu_f, wd_f, wg_s, wu_s, wd_s, sem, *, layer):
    w = pl.program_id(0)
    lo = lo_ref[w]
    hi = hi_ref[w]
    prev = jnp.maximum(w - 1, 0)

    def weight_copies(e):
        return (pltpu.make_async_copy(wg_hbm.at[layer, e], wg_f, sem.at[0]),
                pltpu.make_async_copy(wu_hbm.at[layer, e], wu_f, sem.at[1]),
                pltpu.make_async_copy(wd_hbm.at[layer, e], wd_f, sem.at[2]))

    @pl.when(hi > lo)
    def _():
        @pl.when(first_ref[w] == 1)
        def _():
            @pl.when(w == 0)
            def _():
                for cp in weight_copies(e_ref[0]):
                    cp.start()

            for cp in weight_copies(e_ref[w]):
                cp.wait()
            wg_s[...] = wg_f[...].astype(bf16)
            wu_s[...] = wu_f[...].astype(bf16)
            wd_s[...] = wd_f[...].astype(bf16)

            @pl.when(nxt_ref[w] >= 0)
            def _():
                for cp in weight_copies(nxt_ref[w]):
                    cp.start()

        xb = x_ref[...].astype(bf16)
        gate = jnp.dot(xb, wg_s[...], preferred_element_type=f32)
        up = jnp.dot(xb, wu_s[...], preferred_element_type=f32)
        mid = (_silu(gate) * up).astype(bf16)
        y = jnp.dot(mid, wd_s[...], preferred_element_type=f32)
        r = lax.broadcasted_iota(i32, (y.shape[0], 1), 0)
        mine = (r >= lo) & (r < hi)
        first_visit = (w == 0) | (blk_ref[w] != blk_ref[prev])

        @pl.when(first_visit)
        def _():
            y_ref[...] = jnp.where(mine, y, 0.0)

        @pl.when(jnp.logical_not(first_visit))
        def _():
            y_ref[...] = jnp.where(mine, y, y_ref[...])


def _moe_ffn(xs, items, layer, w_gate, w_up, w_down):
    rows, d = xs.shape
    bm = MOE_BLOCK
    ff = w_gate.shape[3]
    n_items = items[0].shape[0]
    grid_spec = pltpu.PrefetchScalarGridSpec(
        num_scalar_prefetch=6,
        grid=(n_items,),
        in_specs=[
            pl.BlockSpec((bm, d), lambda w, blk, *_: (blk[w], 0)),
            pl.BlockSpec(memory_space=pl.ANY),
            pl.BlockSpec(memory_space=pl.ANY),
            pl.BlockSpec(memory_space=pl.ANY),
        ],
        out_specs=pl.BlockSpec((bm, d), lambda w, blk, *_: (blk[w], 0)),
        scratch_shapes=[pltpu.VMEM((d, ff), f32), pltpu.VMEM((d, ff), f32), pltpu.VMEM((ff, d), f32),
                        pltpu.VMEM((d, ff), bf16), pltpu.VMEM((d, ff), bf16), pltpu.VMEM((ff, d), bf16),
                        pltpu.SemaphoreType.DMA((3,))],
    )
    return pl.pallas_call(
        functools.partial(_ffn_kernel, layer=layer),
        grid_spec=grid_spec,
        out_shape=jax.ShapeDtypeStruct((rows, d), f32),
        compiler_params=_cparams(1),
        name="moe_ffn",
    )(*items, xs, w_gate, w_up, w_down)


def _combine_kernel(pos_cur_ref, pos_nxt_ref, x_ref, gt_ref, ng_ref, nsc_ref, nsh_ref, route_ref, ys_hbm, *rest, final):
    out_refs, (ybuf, sem) = rest[:-2], rest[-2:]
    i = pl.program_id(0)
    n = pl.num_programs(0)
    tm = x_ref.shape[1]
    slot = lax.rem(i, 2)

    @pl.when(i == 0)
    def _():
        _row_gather_start(pos_cur_ref, TOP_K * tm, ys_hbm, ybuf.at[0], sem.at[0])

    @pl.when(i + 1 < n)
    def _():
        _row_gather_start(pos_nxt_ref, TOP_K * tm, ys_hbm, ybuf.at[1 - slot], sem.at[1 - slot])

    _row_gather_wait(TOP_K * tm, ys_hbm, ybuf.at[slot], sem.at[slot])
    route = route_ref[0]
    f = ybuf[slot, 0:tm, :] * route[:, TOP_K:TOP_K + 1] + ybuf[slot, tm:2 * tm, :] * route[:, TOP_K + 1:TOP_K + 2]
    xo = x_ref[0] + gt_ref[0] * f
    hn = _rms_mod(xo, ng_ref[...], nsc_ref[0], nsh_ref[0])
    if final:
        out_refs[0][0] = hn
    else:
        out_refs[0][0] = xo
        out_refs[1][0] = hn.astype(out_refs[1].dtype)


def _moe_combine(x1, gt, norm_g, norm_sc, norm_sh, route, ys, pos, tm, final):
    b, l, d = x1.shape
    nt = (b * l) // tm
    per_b = l // tm
    pos3 = _pos_blocks(pos, tm)
    tile = pl.BlockSpec((1, tm, d), lambda i: (i // per_b, i % per_b, 0))
    per_b_vec = pl.BlockSpec((1, 1, d), lambda i: (i // per_b, 0, 0))
    if final:
        out_specs, out_shape = tile, jax.ShapeDtypeStruct((b, l, d), f32)
    else:
        out_specs = [tile, tile]
        out_shape = [jax.ShapeDtypeStruct((b, l, d), f32), jax.ShapeDtypeStruct((b, l, d), bf16)]
    return pl.pallas_call(
        functools.partial(_combine_kernel, final=final),
        grid=(nt,),
        in_specs=[
            pl.BlockSpec((1, 1, TOP_K * tm), lambda i: (i, 0, 0), memory_space=pltpu.SMEM),
            pl.BlockSpec((1, 1, TOP_K * tm), lambda i: (jnp.minimum(i + 1, nt - 1), 0, 0), memory_space=pltpu.SMEM),
            tile, per_b_vec,
            pl.BlockSpec((1, d), lambda i: (0, 0)),
            per_b_vec, per_b_vec,
            pl.BlockSpec((1, tm, ROUTE_LANES), lambda i: (i // per_b, i % per_b, 0)),
            pl.BlockSpec(memory_space=pl.ANY),
        ],
        out_specs=out_specs,
        out_shape=out_shape,
        scratch_shapes=[pltpu.VMEM((2, TOP_K * tm, d), f32), pltpu.SemaphoreType.DMA((2,))],
        compiler_params=_cparams(1),
        name="moe_combine",
    )(pos3, pos3, x1, gt, norm_g.reshape(1, d), norm_sc, norm_sh, route, ys)


def _dispatch_plan(route, cnt, n_tok):
    bm = MOE_BLOCK
    n_rows = n_tok * TOP_K
    n_blk = n_rows // bm
    experts = jnp.arange(N_EXPERTS, dtype=i32)
    counts = cnt[0, N_EXPERT_GROUPS:N_EXPERT_GROUPS + N_EXPERTS].astype(i32)
    ends = jnp.cumsum(counts)
    starts = ends - counts
    eid = route[..., 0:TOP_K].astype(i32).reshape(n_tok, TOP_K)
    rank = route[..., 2 * TOP_K:3 * TOP_K].astype(i32).reshape(n_tok, TOP_K)
    pos = rank + jnp.sum(jnp.where(eid[..., None] == experts, starts, 0), axis=-1)
    first_blk = starts // bm
    n_it = jnp.where(counts > 0, (ends - 1) // bm - first_blk + 1, 0)
    it_end = jnp.cumsum(n_it)
    it_start = it_end - n_it
    w = jnp.arange(n_blk + N_EXPERTS, dtype=i32)
    live = w < it_end[-1]
    e_last = jnp.max(jnp.where(counts > 0, experts, 0))
    e_w = jnp.where(live, jnp.sum((it_end[None, :] <= w[:, None]).astype(i32), axis=1), e_last)
    hot = e_w[:, None] == experts
    pick = lambda tab: jnp.sum(jnp.where(hot, tab, 0), axis=1)
    blk_w = jnp.where(live, pick(first_blk) + w - pick(it_start), n_blk - 1)
    lo_w = jnp.where(live, jnp.maximum(pick(starts), blk_w * bm) - blk_w * bm, 0)
    hi_w = jnp.where(live, jnp.minimum(pick(ends), (blk_w + 1) * bm) - blk_w * bm, 0)
    first_w = (live & ((w == 0) | (e_w != jnp.roll(e_w, 1)))).astype(i32)
    later = (experts[None, :] > experts[:, None]) & (counts[None, :] > 0)
    nxt_tab = jnp.min(jnp.where(later, experts[None, :], N_EXPERTS), axis=1)
    nxt_w = jnp.where(live, pick(jnp.where(nxt_tab < N_EXPERTS, nxt_tab, -1)), -1)
    return pos, (blk_w, e_w, lo_w, hi_w, first_w, nxt_w)


def _hier_moe(x1, h2, route, cnt, gt, norm_g, norm_sc, norm_sh, layer, w_gate, w_up, w_down, final):
    b, l, d = x1.shape
    n_tok = b * l
    assert (n_tok * TOP_K) % MOE_BLOCK == 0
    pos, items = _dispatch_plan(route, cnt, n_tok)
    xs = _moe_dispatch(h2.reshape(n_tok, d), pos, 512)
    ys = _moe_ffn(xs, items, layer, w_gate, w_up, w_down)
    return _moe_combine(x1, gt, norm_g, norm_sc, norm_sh, route, ys, pos, 256, final)


def _router_params(rg_w, rg_b, re_w, re_b):
    d = rg_w.shape[0]
    pad = ROUTE_LANES - N_EXPERT_GROUPS - N_EXPERTS
    wr = jnp.concatenate([rg_w, re_w, jnp.zeros((d, pad), f32)], axis=1).astype(bf16)
    br = jnp.concatenate([rg_b, re_b, jnp.zeros((pad,), f32)]).reshape(1, ROUTE_LANES)
    return wr, br


def kernel(x, c, ctx, c_ctx, w_mod, b_mod, norm_mix_g, norm_ffn_g, final_g, even_w_in, even_w_out, hgrn_lb_raw,
           hgrn_norm_g, gmlp_ln_g, gmlp_ln_b, gmlp_w_s, gmlp_b_s, odd_w_in, odd_conv_w, odd_w_out, router_g_w,
           router_g_b, router_e_w, router_e_b, exp_w_gate, exp_w_up, exp_w_down):
    b, l, d = x.shape
    depth = w_mod.shape[0]
    assert depth == 2, "layer plan below is written for one even and one odd layer"
    width_a = hgrn_norm_g.shape[1]
    lc = ctx.shape[1]

    rows = -(-(b + 1) // 8) * 8
    c_all = jnp.zeros((rows, d), f32).at[:b].set(c).at[b].set(c_ctx)
    mod = _mod(c_all, w_mod, b_mod)

    def latent_mod(layer):
        return [m.reshape(b, 1, d) for m in jnp.split(mod[layer, :b], 6, axis=-1)]

    sh_m, sc_m, gt_m, sh_f, sc_f, gt_f = latent_mod(0)
    csh_m = jnp.broadcast_to(mod[0, b, 0:d].reshape(1, 1, d), (b, 1, d))
    csc_m = jnp.broadcast_to(mod[0, b, d:2 * d].reshape(1, 1, d), (b, 1, d))
    w_in = even_w_in[0].astype(bf16)
    proj = _norm_matmul(x, norm_mix_g[0], sc_m, sh_m, w_in, min(l, 1024), 1024)
    proj_c = _norm_matmul(ctx, norm_mix_g[0], csc_m, csh_m, w_in[:, width_a:4 * width_a], lc, 1024)
    lb = jnp.cumsum(jax.nn.softmax(hgrn_lb_raw.astype(f32), axis=0), axis=0)[0]
    a = _hgrn(proj, proj_c, lb, hgrn_norm_g[0], width_a)
    wr, br = _router_params(router_g_w[0], router_g_b[0], router_e_w[0], router_e_b[0])
    x1, h2, route, cnt = _even_out(x, a, proj, gmlp_ln_g[0], gmlp_ln_b[0], gmlp_w_s[0].astype(bf16), gmlp_b_s[0].T,
                                   even_w_out[0].astype(bf16), gt_m, norm_ffn_g[0], sc_f, sh_f, wr, br)
    sh_m1, sc_m1, gt_m1, sh_f1, sc_f1, gt_f1 = latent_mod(1)
    x, h = _hier_moe(x1, h2, route, cnt, gt_f, norm_mix_g[1], sc_m1, sh_m1, 0, exp_w_gate, exp_w_up, exp_w_down, False)

    pc = _matmul(h.reshape(b * l, d), odd_w_in[0].astype(bf16), 2048, 1024).reshape(b, l, -1)
    wr, br = _router_params(router_g_w[1], router_g_b[1], router_e_w[1], router_e_b[1])
    x1, h2, route, cnt = _odd_out(x, pc, odd_conv_w[0], odd_w_out[0].astype(bf16), gt_m1, norm_ffn_g[1], sc_f1, sh_f1,
                                  wr, br)
    zero = jnp.zeros((b, 1, d), f32)
    return _hier_moe(x1, h2, route, cnt, gt_f1, final_g, zero, zero, 1, exp_w_gate, exp_w_up, exp_w_down, True)
```

```python
import functools

import jax
import jax.numpy as jnp
from jax import lax
from jax.experimental import pallas as pl
from jax.experimental.pallas import tpu as pltpu

f32 = jnp.float32
bf16 = jnp.bfloat16
i32 = jnp.int32

EPS = 1e-6
HEAD_DIM = 128
SCAN_CHUNK = 64
GATE_ROWS = 128
CHUNK_B = 128
N_GROUPS_B = 8
GRID_W = 64
N_EXPERT_GROUPS = 4
EXPERTS_PER_GROUP = 8
N_EXPERTS = N_EXPERT_GROUPS * EXPERTS_PER_GROUP
TOP_K = 2
ROUTE_LANES = 128
MOE_BLOCK = 256
SUB = 256
VMEM_LIMIT = 56 * 1024 * 1024

NT = (((1,), (1,)), ((), ()))
TN = (((0,), (0,)), ((), ()))


def _cparams(n_axes):
    return pltpu.CompilerParams(dimension_semantics=("arbitrary",) * n_axes, vmem_limit_bytes=VMEM_LIMIT)


def _silu(v):
    return v * jax.nn.sigmoid(v)


def _mod_kernel(c_ref, w_ref, b_ref, o_ref):
    s = _silu(c_ref[...])
    o_ref[0] = jnp.dot(s, w_ref[0], preferred_element_type=f32, precision=lax.Precision.HIGHEST) + b_ref[0]


def _mod(c_all, w_mod, b_mod):
    depth, d, n = w_mod.shape
    rows = c_all.shape[0]
    tn = 1024
    return pl.pallas_call(
        _mod_kernel,
        grid=(depth, n // tn),
        in_specs=[
            pl.BlockSpec((rows, d), lambda l, j: (0, 0)),
            pl.BlockSpec((1, d, tn), lambda l, j: (l, 0, j)),
            pl.BlockSpec((1, 1, tn), lambda l, j: (l, 0, j)),
        ],
        out_specs=pl.BlockSpec((1, rows, tn), lambda l, j: (l, 0, j)),
        out_shape=jax.ShapeDtypeStruct((depth, rows, n), f32),
        compiler_params=_cparams(2),
        name="mod",
    )(c_all, w_mod, b_mod.reshape(depth, 1, n))


def _rms_mod(x, g, sc, sh):
    r = lax.rsqrt(jnp.sum(x * x, axis=-1, keepdims=True) * (1.0 / x.shape[-1]) + EPS)
    return (x * r) * (g * (1.0 + sc)) + sh


def _norm_matmul_kernel(x_ref, g_ref, sc_ref, sh_ref, w_ref, o_ref, h_ref):
    @pl.when(pl.program_id(2) == 0)
    def _():
        h_ref[...] = _rms_mod(x_ref[0], g_ref[...], sc_ref[0], sh_ref[0]).astype(bf16)

    o_ref[0] = jnp.dot(h_ref[...], w_ref[...], preferred_element_type=f32).astype(o_ref.dtype)


def _norm_matmul(x, g, sc, sh, w, tm, tn):
    b, l, d = x.shape
    n = w.shape[1]
    return pl.pallas_call(
        _norm_matmul_kernel,
        grid=(b, l // tm, n // tn),
        in_specs=[
            pl.BlockSpec((1, tm, d), lambda bi, i, j: (bi, i, 0)),
            pl.BlockSpec((1, d), lambda bi, i, j: (0, 0)),
            pl.BlockSpec((1, 1, d), lambda bi, i, j: (bi, 0, 0)),
            pl.BlockSpec((1, 1, d), lambda bi, i, j: (bi, 0, 0)),
            pl.BlockSpec((d, tn), lambda bi, i, j: (0, j)),
        ],
        out_specs=pl.BlockSpec((1, tm, tn), lambda bi, i, j: (bi, i, j)),
        out_shape=jax.ShapeDtypeStruct((b, l, n), bf16),
        scratch_shapes=[pltpu.VMEM((tm, d), bf16)],
        compiler_params=_cparams(3),
        name="norm_matmul",
    )(x, g.reshape(1, d), sc, sh, w)


def _matmul_kernel(x_ref, w_ref, o_ref):
    o_ref[...] = jnp.dot(x_ref[...], w_ref[...], preferred_element_type=f32).astype(o_ref.dtype)


def _matmul(x, w, tm, tn):
    m, d = x.shape
    n = w.shape[1]
    tm = min(tm, m)
    return pl.pallas_call(
        _matmul_kernel,
        grid=(m // tm, n // tn),
        in_specs=[pl.BlockSpec((tm, d), lambda i, j: (i, 0)), pl.BlockSpec((d, tn), lambda i, j: (0, j))],
        out_specs=pl.BlockSpec((tm, tn), lambda i, j: (i, j)),
        out_shape=jax.ShapeDtypeStruct((m, n), bf16),
        compiler_params=_cparams(2),
        name="matmul",
    )(x, w)


def _hgrn_kernel(q_ref, ff_ref, fb_ref, v_ref, g_ref, cff_ref, cfb_ref, cv_ref, lb_ref, ng_ref,
                 a_ref, of_ref, ob_ref, qd_ref, kd_ref, ks_ref, dec_ref, st_ref):
    C = SCAN_CHUNK
    G = GATE_ROWS
    L = q_ref.shape[1]
    Lc = cff_ref.shape[1]
    n = L // C
    row = lax.broadcasted_iota(i32, (C, C), 0)
    col = lax.broadcasted_iota(i32, (C, C), 1)
    lower = col <= row
    upper = col >= row
    grow = lax.broadcasted_iota(i32, (G, G), 0)
    gcol = lax.broadcasted_iota(i32, (G, G), 1)
    same_chunk = (grow // C) == (gcol // C)
    tri = ((same_chunk & (gcol <= grow)).astype(bf16), (same_chunk & (gcol >= grow)).astype(bf16))
    lbs = (lb_ref[0:1, :], lb_ref[1:2, :])

    def gate_block(f_raw, d):
        f = lbs[d] + (1.0 - lbs[d]) * jax.nn.sigmoid(f_raw.astype(f32))
        logf = jnp.log(f)
        hi = logf.astype(bf16)
        low = (logf - hi.astype(f32)).astype(bf16)
        cum = jnp.dot(tri[d], hi, preferred_element_type=f32) + jnp.dot(tri[d], low, preferred_element_type=f32)
        k3 = (1.0 - f).reshape(G // C, C, HEAD_DIM)
        cum3 = cum.reshape(G // C, C, HEAD_DIM)
        edge3 = cum3[:, C - 1:C, :] if d == 0 else cum3[:, 0:1, :]
        k_dec = (k3 * jnp.exp(-cum3)).reshape(G, HEAD_DIM).astype(bf16)
        k_st = (k3 * jnp.exp(edge3 - cum3)).reshape(G, HEAD_DIM).astype(bf16)
        dec = [jnp.exp(edge3[c]) for c in range(G // C)]
        return cum, k_dec, k_st, dec

    f_refs = (cff_ref, cfb_ref)
    for d in range(2):
        st = jnp.zeros((HEAD_DIM, HEAD_DIM), f32)
        blocks = range(Lc // G) if d == 0 else range(Lc // G - 1, -1, -1)
        for gb in blocks:
            _, _, k_st, dec = gate_block(f_refs[d][0, gb * G:(gb + 1) * G, :], d)
            chunks = range(G // C) if d == 0 else range(G // C - 1, -1, -1)
            for c in chunks:
                v = cv_ref[0, gb * G + c * C:gb * G + (c + 1) * C, :]
                st = (dec[c] * st
                      + lax.dot_general(v, k_st[c * C:(c + 1) * C, :], TN, preferred_element_type=f32))
        st_ref[d] = st

    def precompute(i, carry):
        off = pl.multiple_of(i * G, G)
        rows = pl.ds(off, G)
        qs = _silu(q_ref[0, rows, :].astype(f32))
        for d, f_ref in enumerate((ff_ref, fb_ref)):
            cum, k_dec, k_st, dec = gate_block(f_ref[0, rows, :], d)
            qd_ref[d, rows, :] = (qs * jnp.exp(cum)).astype(bf16)
            kd_ref[d, rows, :] = k_dec
            ks_ref[d, rows, :] = k_st
            for c in range(G // C):
                dec_ref[d, pl.ds(off + c * C, 1), :] = dec[c]
        return carry

    lax.fori_loop(0, L // G, precompute, 0, unroll=4)

    def chunk_step(off, d, mask, o_ref):
        rows = pl.ds(off, C)
        qd = qd_ref[d, rows, :]
        v = v_ref[0, rows, :]
        s = lax.dot_general(qd, kd_ref[d, rows, :], NT, preferred_element_type=f32)
        s = jnp.where(mask, s, 0.0).astype(bf16)
        st = st_ref[d]
        o_ref[rows, :] = (jnp.dot(s, v, preferred_element_type=f32)
                          + lax.dot_general(qd, st.astype(bf16), NT, preferred_element_type=f32))
        st_ref[d] = (dec_ref[d, pl.ds(off, 1), :] * st
                     + lax.dot_general(v, ks_ref[d, rows, :], TN, preferred_element_type=f32))

    def body(c, carry):
        chunk_step(pl.multiple_of(c * C, C), 0, lower, of_ref)
        chunk_step(pl.multiple_of((n - 1 - c) * C, C), 1, upper, ob_ref)
        return carry

    lax.fori_loop(0, n, body, 0, unroll=8)

    R = 512 if L % 512 == 0 else C

    def epilogue(r, carry):
        rows = pl.ds(pl.multiple_of(r * R, R), R)
        o = of_ref[rows, :] + ob_ref[rows, :]
        o = o * lax.rsqrt(jnp.mean(o * o, axis=-1, keepdims=True) + EPS)
        o = o * ng_ref[...]
        a_ref[0, rows, :] = (o * _silu(g_ref[0, rows, :].astype(f32))).astype(a_ref.dtype)
        return carry

    lax.fori_loop(0, L // R, epilogue, 0)


def _hgrn(proj, proj_c, lb, norm_g, width_a):
    b, l, _ = proj.shape
    lc = proj_c.shape[1]
    nh = width_a // HEAD_DIM

    def seg(k):
        return pl.BlockSpec((1, l, HEAD_DIM), lambda bi, h, k=k: (bi, 0, k * nh + h))

    def cseg(k):
        return pl.BlockSpec((1, lc, HEAD_DIM), lambda bi, h, k=k: (bi, 0, k * nh + h))

    return pl.pallas_call(
        _hgrn_kernel,
        grid=(b, nh),
        in_specs=[seg(0), seg(1), seg(2), seg(3), seg(4), cseg(0), cseg(1), cseg(2),
                  pl.BlockSpec((2, HEAD_DIM), lambda bi, h: (0, h)),
                  pl.BlockSpec((1, HEAD_DIM), lambda bi, h: (0, h))],
        out_specs=pl.BlockSpec((1, l, HEAD_DIM), lambda bi, h: (bi, 0, h)),
        out_shape=jax.ShapeDtypeStruct((b, l, width_a), bf16),
        scratch_shapes=[pltpu.VMEM((l, HEAD_DIM), f32), pltpu.VMEM((l, HEAD_DIM), f32),
                        pltpu.VMEM((2, l, HEAD_DIM), bf16), pltpu.VMEM((2, l, HEAD_DIM), bf16),
                        pltpu.VMEM((2, l, HEAD_DIM), bf16), pltpu.VMEM((2, l, HEAD_DIM), f32),
                        pltpu.VMEM((2, HEAD_DIM, HEAD_DIM), f32)],
        compiler_params=_cparams(2),
        name="hgrn",
    )(proj, proj, proj, proj, proj, proj_c, proj_c, proj_c, lb, norm_g.reshape(1, width_a))


def _residual_norm_route(rows, valid, x, y, gt, g2, sc, sh, wr_ref, br_ref, x1_ref, h2_ref, route_ref, cnt_ref,
                         cnt_acc):
    x1 = x + gt * y
    x1_ref[0, rows, :] = x1
    h2 = _rms_mod(x1, g2, sc, sh)
    h2_ref[0, rows, :] = h2
    logits = jnp.dot(h2.astype(bf16), wr_ref[...], preferred_element_type=f32) + br_ref[...]
    tm = logits.shape[0]
    lane = lax.broadcasted_iota(i32, (tm, ROUTE_LANES), 1)
    lane_f = lane.astype(f32)
    neg = -jnp.inf
    gl = jnp.where(lane < N_EXPERT_GROUPS, logits, neg)
    gmax = jnp.max(gl, axis=-1, keepdims=True)
    p_top = 1.0 / jnp.sum(jnp.exp(gl - gmax), axis=-1, keepdims=True)
    grp = jnp.min(jnp.where(gl == gmax, lane_f, float(ROUTE_LANES)), axis=-1, keepdims=True)
    lo = float(N_EXPERT_GROUPS) + grp * float(EXPERTS_PER_GROUP)
    emask = (lane_f >= lo) & (lane_f < lo + float(EXPERTS_PER_GROUP))
    el = jnp.where(emask, logits, neg)
    emax = jnp.max(el, axis=-1, keepdims=True)
    esum = jnp.sum(jnp.exp(el - emax), axis=-1, keepdims=True)
    i1 = jnp.min(jnp.where(el == emax, lane_f, float(ROUTE_LANES)), axis=-1, keepdims=True)
    p1 = 1.0 / esum
    el2 = jnp.where(lane_f == i1, neg, el)
    emax2 = jnp.max(el2, axis=-1, keepdims=True)
    i2 = jnp.min(jnp.where(el2 == emax2, lane_f, float(ROUTE_LANES)), axis=-1, keepdims=True)
    p2 = jnp.exp(emax2 - emax) / esum
    den = p1 + p2
    w1 = p_top * p1 / den
    w2 = p_top * p2 / den
    e1 = i1 - float(N_EXPERT_GROUPS)
    e2 = i2 - float(N_EXPERT_GROUPS)
    hot1 = lane_f == i1
    hot2 = lane_f == i2
    both = jnp.where((hot1 | hot2) & valid, 1.0, 0.0)
    trow = lax.broadcasted_iota(i32, (tm, tm), 0)
    tcol = lax.broadcasted_iota(i32, (tm, tm), 1)
    before = (tcol < trow).astype(bf16)
    base = jnp.dot(before, both.astype(bf16), preferred_element_type=f32) + cnt_acc[...]
    r1 = jnp.sum(jnp.where(hot1, base, 0.0), axis=-1, keepdims=True)
    r2 = jnp.sum(jnp.where(hot2, base, 0.0), axis=-1, keepdims=True)
    cnt = cnt_acc[...] + jnp.sum(both, axis=0, keepdims=True)
    cnt_acc[...] = cnt
    cnt_ref[...] = cnt
    route = jnp.zeros((tm, ROUTE_LANES), f32)
    for k, val in enumerate((e1, e2, w1, w2, r1, r2)):
        route = jnp.where(lane == k, val, route)
    route_ref[0, rows, :] = route


def _out_pipeline(stage_fn, x_ref, wo_ref, gt_ref, g2_ref, sc_ref, sh_ref, wr_ref, br_ref,
                  x1_ref, h2_ref, route_ref, cnt_ref, t_bufs, y_bufs, cnt_acc):
    p = pl.program_id(0)

    @pl.when(p == 0)
    def _():
        for ref in (*t_bufs, *y_bufs, cnt_acc):
            ref[...] = jnp.zeros_like(ref)

    for half in range(2):
        rows = slice(half * SUB, (half + 1) * SUB)
        _residual_norm_route(rows, p > 0, x_ref[0, rows, :], y_bufs[half][...], gt_ref[0], g2_ref[...], sc_ref[0],
                             sh_ref[0], wr_ref, br_ref, x1_ref, h2_ref, route_ref, cnt_ref, cnt_acc)
        y_bufs[1 - half][...] = jnp.dot(t_bufs[1 - half][...], wo_ref[...], preferred_element_type=f32)
        stage_fn(rows, t_bufs[half])


def _gelu_tanh(v):
    return 0.5 * v * (1.0 + jnp.tanh(0.7978845608028654 * (v + 0.044715 * (v * v * v))))


def _even_out_kernel(a_ref, u_ref, v_ref, lng_ref, lnb_ref, ws_ref, bs_ref, x_ref, wo_ref, gt_ref, g2_ref, sc_ref,
                     sh_ref, wr_ref, br_ref, x1_ref, h2_ref, route_ref, cnt_ref, t0, t1, y0, y1, cnt_acc):
    wa = a_ref.shape[2]
    gd = u_ref.shape[2] // N_GROUPS_B

    def stage(rows, t_ref):
        t_ref[:, 0:wa] = a_ref[0, rows, :]
        gv = _gelu_tanh(v_ref[0, rows, :].astype(f32))
        mu = jnp.mean(gv, axis=-1, keepdims=True)
        dv = gv - mu
        var = jnp.mean(dv * dv, axis=-1, keepdims=True)
        vn = (dv * lax.rsqrt(var + EPS) * lng_ref[...] + lnb_ref[...]).astype(bf16)
        for nb in range(SUB // CHUNK_B):
            rs = slice(nb * CHUNK_B, (nb + 1) * CHUNK_B)
            ru = slice(rows.start + nb * CHUNK_B, rows.start + (nb + 1) * CHUNK_B)
            for g in range(N_GROUPS_B):
                cs = slice(g * gd, (g + 1) * gd)
                s = jnp.dot(ws_ref[g], vn[rs, cs], preferred_element_type=f32) + bs_ref[:, g:g + 1]
                t_ref[rs, wa + g * gd:wa + (g + 1) * gd] = (_gelu_tanh(u_ref[0, ru, cs].astype(f32)) * s).astype(bf16)

    _out_pipeline(stage, x_ref, wo_ref, gt_ref, g2_ref, sc_ref, sh_ref, wr_ref, br_ref,
                  x1_ref, h2_ref, route_ref, cnt_ref, (t0, t1), (y0, y1), cnt_acc)


def _odd_out_kernel(hx_ref, bg_ref, cg_ref, cw_ref, x_ref, wo_ref, gt_ref, g2_ref, sc_ref,
                    sh_ref, wr_ref, br_ref, x1_ref, h2_ref, route_ref, cnt_ref, t0, t1, y0, y1, cnt_acc):
    def stage(rows, t_ref):
        z = cg_ref[0, rows, :].astype(f32) * hx_ref[0, rows, :].astype(f32)
        pos = lax.broadcasted_iota(i32, (SUB, 1), 0) % GRID_W
        z_prev = jnp.where(pos == 0, 0.0, pltpu.roll(z, 1, 0))
        z_next = jnp.where(pos == GRID_W - 1, 0.0, pltpu.roll(z, SUB - 1, 0))
        conv = z_prev * cw_ref[0:1, :] + z * cw_ref[1:2, :] + z_next * cw_ref[2:3, :]
        t_ref[...] = (bg_ref[0, rows, :].astype(f32) * conv).astype(bf16)

    _out_pipeline(stage, x_ref, wo_ref, gt_ref, g2_ref, sc_ref, sh_ref, wr_ref, br_ref,
                  x1_ref, h2_ref, route_ref, cnt_ref, (t0, t1), (y0, y1), cnt_acc)


def _mixer_out(kernel_fn, name, stage_args, stage_specs, x, w_out, gt, g2, sc, sh, wr, br):
    b, l, d = x.shape
    blk = 2 * SUB
    per_b = l // blk
    n = b * per_b
    k = w_out.shape[0]

    def ahead(p):
        q = jnp.minimum(p, n - 1)
        return q // per_b, q % per_b

    def behind(p):
        q = jnp.maximum(p - 1, 0)
        return q // per_b, q % per_b

    const = lambda shape: pl.BlockSpec(shape, lambda p: (0,) * len(shape))
    per_b_vec = pl.BlockSpec((1, 1, d), lambda p: (behind(p)[0], 0, 0))
    tile = lambda width: pl.BlockSpec((1, blk, width), lambda p: (*behind(p), 0))
    return pl.pallas_call(
        kernel_fn,
        grid=(n + 1,),
        in_specs=stage_specs(ahead, blk, const) + [
            tile(d), const((k, d)), per_b_vec, const((1, d)), per_b_vec, per_b_vec,
            const((d, ROUTE_LANES)), const((1, ROUTE_LANES))],
        out_specs=[tile(d), tile(d), tile(ROUTE_LANES), const((1, ROUTE_LANES))],
        out_shape=[jax.ShapeDtypeStruct((b, l, d), f32), jax.ShapeDtypeStruct((b, l, d), f32),
                   jax.ShapeDtypeStruct((b, l, ROUTE_LANES), f32), jax.ShapeDtypeStruct((1, ROUTE_LANES), f32)],
        scratch_shapes=[pltpu.VMEM((SUB, k), bf16), pltpu.VMEM((SUB, k), bf16),
                        pltpu.VMEM((SUB, d), f32), pltpu.VMEM((SUB, d), f32), pltpu.VMEM((1, ROUTE_LANES), f32)],
        compiler_params=_cparams(1),
        name=name,
    )(*stage_args, x, w_out, gt, g2.reshape(1, d), sc, sh, wr, br)


def _even_out(x, a, proj, ln_g, ln_b, w_s, b_s_t, w_out, gt, g2, sc, sh, wr, br):
    wa = a.shape[2]
    wb = ln_g.shape[0]
    ub = (5 * wa) // wb

    def stage_specs(ahead, blk, const):
        return [pl.BlockSpec((1, blk, wa), lambda p: (*ahead(p), 0)),
                pl.BlockSpec((1, blk, wb), lambda p: (*ahead(p), ub)),
                pl.BlockSpec((1, blk, wb), lambda p: (*ahead(p), ub + 1)),
                const((1, wb)), const((1, wb)), const((N_GROUPS_B, CHUNK_B, CHUNK_B)), const((CHUNK_B, N_GROUPS_B))]

    return _mixer_out(_even_out_kernel, "even_out",
                      (a, proj, proj, ln_g.reshape(1, wb), ln_b.reshape(1, wb), w_s, b_s_t), stage_specs,
                      x, w_out, gt, g2, sc, sh, wr, br)


def _odd_out(x, pc, conv_w, w_out, gt, g2, sc, sh, wr, br):
    wc = w_out.shape[0]

    def stage_specs(ahead, blk, const):
        return [pl.BlockSpec((1, blk, wc), lambda p, j=j: (*ahead(p), j)) for j in range(3)] + [const(conv_w.shape)]

    return _mixer_out(_odd_out_kernel, "odd_out", (pc, pc, pc, conv_w), stage_specs,
                      x, w_out, gt, g2, sc, sh, wr, br)


def _row_gather_start(idx_ref, n_rows, src_hbm, dst_buf, sem):
    def body(r, carry):
        t = idx_ref[0, 0, r]
        pltpu.make_async_copy(src_hbm.at[pl.ds(t, 1), :], dst_buf.at[pl.ds(r, 1), :], sem).start()
        return carry

    lax.fori_loop(0, n_rows, body, 0, unroll=8)


def _row_gather_wait(n_rows, src_hbm, dst_buf, sem):
    def body(r, carry):
        pltpu.make_async_copy(src_hbm.at[pl.ds(0, 1), :], dst_buf.at[pl.ds(r, 1), :], sem).wait()
        return carry

    lax.fori_loop(0, n_rows, body, 0, unroll=8)


def _dispatch_kernel(pos_ref, h_ref, xs_hbm, sem):
    tm = h_ref.shape[0]

    def start(r, carry):
        for k in range(TOP_K):
            pltpu.make_async_copy(h_ref.at[pl.ds(r, 1), :], xs_hbm.at[pl.ds(pos_ref[0, 0, k * tm + r], 1), :],
                                  sem.at[0]).start()
        return carry

    lax.fori_loop(0, tm, start, 0, unroll=8)

    def wait(r, carry):
        pltpu.make_async_copy(h_ref.at[pl.ds(0, 1), :], xs_hbm.at[pl.ds(0, 1), :], sem.at[0]).wait()
        return carry

    lax.fori_loop(0, TOP_K * tm, wait, 0, unroll=8)


def _pos_blocks(pos, tm):
    nt = pos.shape[0] // tm
    return pos.reshape(nt, tm, TOP_K).transpose(0, 2, 1).reshape(nt, 1, TOP_K * tm)


def _moe_dispatch(h2, pos, tm):
    t, d = h2.shape
    return pl.pallas_call(
        _dispatch_kernel,
        grid=(t // tm,),
        in_specs=[pl.BlockSpec((1, 1, TOP_K * tm), lambda i: (i, 0, 0), memory_space=pltpu.SMEM),
                  pl.BlockSpec((tm, d), lambda i: (i, 0))],
        out_specs=pl.BlockSpec(memory_space=pl.ANY),
        out_shape=jax.ShapeDtypeStruct((t * TOP_K, d), h2.dtype),
        scratch_shapes=[pltpu.SemaphoreType.DMA((1,))],
        compiler_params=_cparams(1),
        name="moe_dispatch",
    )(_pos_blocks(pos, tm), h2)


def _ffn_kernel(blk_ref, e_ref, lo_ref, hi_ref, first_ref, nxt_ref, x_ref, wg_hbm, wu_hbm, wd_hbm, y_ref,
                wg_f, wu_f, wd_f, wg_s, wu_s, wd_s, sem, *, layer):
    w = pl.program_id(0)
    lo = lo_ref[w]
    hi = hi_ref[w]
    prev = jnp.maximum(w - 1, 0)

    def weight_copies(e):
        return (pltpu.make_async_copy(wg_hbm.at[layer, e], wg_f, sem.at[0]),
                pltpu.make_async_copy(wu_hbm.at[layer, e], wu_f, sem.at[1]),
                pltpu.make_async_copy(wd_hbm.at[layer, e], wd_f, sem.at[2]))

    @pl.when(hi > lo)
    def _():
        @pl.when(first_ref[w] == 1)
        def _():
            @pl.when(w == 0)
            def _():
                for cp in weight_copies(e_ref[0]):
                    cp.start()

            for cp in weight_copies(e_ref[w]):
                cp.wait()
            wg_s[...] = wg_f[...].astype(bf16)
            wu_s[...] = wu_f[...].astype(bf16)
            wd_s[...] = wd_f[...].astype(bf16)

            @pl.when(nxt_ref[w] >= 0)
            def _():
                for cp in weight_copies(nxt_ref[w]):
                    cp.start()

        xb = x_ref[...].astype(bf16)
        gate = jnp.dot(xb, wg_s[...], preferred_element_type=f32)
        up = jnp.dot(xb, wu_s[...], preferred_element_type=f32)
        mid = (_silu(gate) * up).astype(bf16)
        y = jnp.dot(mid, wd_s[...], preferred_element_type=f32)
        r = lax.broadcasted_iota(i32, (y.shape[0], 1), 0)
        mine = (r >= lo) & (r < hi)
        first_visit = (w == 0) | (blk_ref[w] != blk_ref[prev])

        @pl.when(first_visit)
        def _():
            y_ref[...] = jnp.where(mine, y, 0.0)

        @pl.when(jnp.logical_not(first_visit))
        def _():
            y_ref[...] = jnp.where(mine, y, y_ref[...])


def _moe_ffn(xs, items, layer, w_gate, w_up, w_down):
    rows, d = xs.shape
    bm = MOE_BLOCK
    ff = w_gate.shape[3]
    n_items = items[0].shape[0]
    grid_spec = pltpu.PrefetchScalarGridSpec(
        num_scalar_prefetch=6,
        grid=(n_items,),
        in_specs=[
            pl.BlockSpec((bm, d), lambda w, blk, *_: (blk[w], 0)),
            pl.BlockSpec(memory_space=pl.ANY),
            pl.BlockSpec(memory_space=pl.ANY),
            pl.BlockSpec(memory_space=pl.ANY),
        ],
        out_specs=pl.BlockSpec((bm, d), lambda w, blk, *_: (blk[w], 0)),
        scratch_shapes=[pltpu.VMEM((d, ff), f32), pltpu.VMEM((d, ff), f32), pltpu.VMEM((ff, d), f32),
                        pltpu.VMEM((d, ff), bf16), pltpu.VMEM((d, ff), bf16), pltpu.VMEM((ff, d), bf16),
                        pltpu.SemaphoreType.DMA((3,))],
    )
    return pl.pallas_call(
        functools.partial(_ffn_kernel, layer=layer),
        grid_spec=grid_spec,
        out_shape=jax.ShapeDtypeStruct((rows, d), f32),
        compiler_params=_cparams(1),
        name="moe_ffn",
    )(*items, xs, w_gate, w_up, w_down)


def _combine_kernel(pos_cur_ref, pos_nxt_ref, x_ref, gt_ref, ng_ref, nsc_ref, nsh_ref, route_ref, ys_hbm, *rest, final):
    out_refs, (ybuf, sem) = rest[:-2], rest[-2:]
    i = pl.program_id(0)
    n = pl.num_programs(0)
    tm = x_ref.shape[1]
    slot = lax.rem(i, 2)

    @pl.when(i == 0)
    def _():
        _row_gather_start(pos_cur_ref, TOP_K * tm, ys_hbm, ybuf.at[0], sem.at[0])

    @pl.when(i + 1 < n)
    def _():
        _row_gather_start(pos_nxt_ref, TOP_K * tm, ys_hbm, ybuf.at[1 - slot], sem.at[1 - slot])

    _row_gather_wait(TOP_K * tm, ys_hbm, ybuf.at[slot], sem.at[slot])
    route = route_ref[0]
    f = ybuf[slot, 0:tm, :] * route[:, TOP_K:TOP_K + 1] + ybuf[slot, tm:2 * tm, :] * route[:, TOP_K + 1:TOP_K + 2]
    xo = x_ref[0] + gt_ref[0] * f
    hn = _rms_mod(xo, ng_ref[...], nsc_ref[0], nsh_ref[0])
    if final:
        out_refs[0][0] = hn
    else:
        out_refs[0][0] = xo
        out_refs[1][0] = hn.astype(out_refs[1].dtype)


def _moe_combine(x1, gt, norm_g, norm_sc, norm_sh, route, ys, pos, tm, final):
    b, l, d = x1.shape
    nt = (b * l) // tm
    per_b = l // tm
    pos3 = _pos_blocks(pos, tm)
    tile = pl.BlockSpec((1, tm, d), lambda i: (i // per_b, i % per_b, 0))
    per_b_vec = pl.BlockSpec((1, 1, d), lambda i: (i // per_b, 0, 0))
    if final:
        out_specs, out_shape = tile, jax.ShapeDtypeStruct((b, l, d), f32)
    else:
        out_specs = [tile, tile]
        out_shape = [jax.ShapeDtypeStruct((b, l, d), f32), jax.ShapeDtypeStruct((b, l, d), bf16)]
    return pl.pallas_call(
        functools.partial(_combine_kernel, final=final),
        grid=(nt,),
        in_specs=[
            pl.BlockSpec((1, 1, TOP_K * tm), lambda i: (i, 0, 0), memory_space=pltpu.SMEM),
            pl.BlockSpec((1, 1, TOP_K * tm), lambda i: (jnp.minimum(i + 1, nt - 1), 0, 0), memory_space=pltpu.SMEM),
            tile, per_b_vec,
            pl.BlockSpec((1, d), lambda i: (0, 0)),
            per_b_vec, per_b_vec,
            pl.BlockSpec((1, tm, ROUTE_LANES), lambda i: (i // per_b, i % per_b, 0)),
            pl.BlockSpec(memory_space=pl.ANY),
        ],
        out_specs=out_specs,
        out_shape=out_shape,
        scratch_shapes=[pltpu.VMEM((2, TOP_K * tm, d), f32), pltpu.SemaphoreType.DMA((2,))],
        compiler_params=_cparams(1),
        name="moe_combine",
    )(pos3, pos3, x1, gt, norm_g.reshape(1, d), norm_sc, norm_sh, route, ys)


def _dispatch_plan(route, cnt, n_tok):
    bm = MOE_BLOCK
    n_rows = n_tok * TOP_K
    n_blk = n_rows // bm
    experts = jnp.arange(N_EXPERTS, dtype=i32)
    counts = cnt[0, N_EXPERT_GROUPS:N_EXPERT_GROUPS + N_EXPERTS].astype(i32)
    ends = jnp.cumsum(counts)
    starts = ends - counts
    eid = route[..., 0:TOP_K].astype(i32).reshape(n_tok, TOP_K)
    rank = route[..., 2 * TOP_K:3 * TOP_K].astype(i32).reshape(n_tok, TOP_K)
    pos = rank + jnp.sum(jnp.where(eid[..., None] == experts, starts, 0), axis=-1)
    first_blk = starts // bm
    n_it = jnp.where(counts > 0, (ends - 1) // bm - first_blk + 1, 0)
    it_end = jnp.cumsum(n_it)
    it_start = it_end - n_it
    w = jnp.arange(n_blk + N_EXPERTS, dtype=i32)
    live = w < it_end[-1]
    e_last = jnp.max(jnp.where(counts > 0, experts, 0))
    e_w = jnp.where(live, jnp.sum((it_end[None, :] <= w[:, None]).astype(i32), axis=1), e_last)
    hot = e_w[:, None] == experts
    pick = lambda tab: jnp.sum(jnp.where(hot, tab, 0), axis=1)
    blk_w = jnp.where(live, pick(first_blk) + w - pick(it_start), n_blk - 1)
    lo_w = jnp.where(live, jnp.maximum(pick(starts), blk_w * bm) - blk_w * bm, 0)
    hi_w = jnp.where(live, jnp.minimum(pick(ends), (blk_w + 1) * bm) - blk_w * bm, 0)
    first_w = (live & ((w == 0) | (e_w != jnp.roll(e_w, 1)))).astype(i32)
    later = (experts[None, :] > experts[:, None]) & (counts[None, :] > 0)
    nxt_tab = jnp.min(jnp.where(later, experts[None, :], N_EXPERTS), axis=1)
    nxt_w = jnp.where(live, pick(jnp.where(nxt_tab < N_EXPERTS, nxt_tab, -1)), -1)
    return pos, (blk_w, e_w, lo_w, hi_w, first_w, nxt_w)


def _hier_moe(x1, h2, route, cnt, gt, norm_g, norm_sc, norm_sh, layer, w_gate, w_up, w_down, final):
    b, l, d = x1.shape
    n_tok = b * l
    assert (n_tok * TOP_K) % MOE_BLOCK == 0
    pos, items = _dispatch_plan(route, cnt, n_tok)
    xs = _moe_dispatch(h2.reshape(n_tok, d), pos, 512)
    ys = _moe_ffn(xs, items, layer, w_gate, w_up, w_down)
    return _moe_combine(x1, gt, norm_g, norm_sc, norm_sh, route, ys, pos, 256, final)


def _router_params(rg_w, rg_b, re_w, re_b):
    d = rg_w.shape[0]
    pad = ROUTE_LANES - N_EXPERT_GROUPS - N_EXPERTS
    wr = jnp.concatenate([rg_w, re_w, jnp.zeros((d, pad), f32)], axis=1).astype(bf16)
    br = jnp.concatenate([rg_b, re_b, jnp.zeros((pad,), f32)]).reshape(1, ROUTE_LANES)
    return wr, br


def kernel(x, c, ctx, c_ctx, w_mod, b_mod, norm_mix_g, norm_ffn_g, final_g, even_w_in, even_w_out, hgrn_lb_raw,
           hgrn_norm_g, gmlp_ln_g, gmlp_ln_b, gmlp_w_s, gmlp_b_s, odd_w_in, odd_conv_w, odd_w_out, router_g_w,
           router_g_b, router_e_w, router_e_b, exp_w_gate, exp_w_up, exp_w_down):
    b, l, d = x.shape
    depth = w_mod.shape[0]
    assert depth == 2, "layer plan below is written for one even and one odd layer"
    width_a = hgrn_norm_g.shape[1]
    lc = ctx.shape[1]

    rows = -(-(b + 1) // 8) * 8
    c_all = jnp.zeros((rows, d), f32).at[:b].set(c).at[b].set(c_ctx)
    mod = _mod(c_all, w_mod, b_mod)

    def latent_mod(layer):
        return [m.reshape(b, 1, d) for m in jnp.split(mod[layer, :b], 6, axis=-1)]

    sh_m, sc_m, gt_m, sh_f, sc_f, gt_f = latent_mod(0)
    csh_m = jnp.broadcast_to(mod[0, b, 0:d].reshape(1, 1, d), (b, 1, d))
    csc_m = jnp.broadcast_to(mod[0, b, d:2 * d].reshape(1, 1, d), (b, 1, d))
    w_in = even_w_in[0].astype(bf16)
    proj = _norm_matmul(x, norm_mix_g[0], sc_m, sh_m, w_in, min(l, 1024), 1024)
    proj_c = _norm_matmul(ctx, norm_mix_g[0], csc_m, csh_m, w_in[:, width_a:4 * width_a], lc, 1024)
    lb = jnp.cumsum(jax.nn.softmax(hgrn_lb_raw.astype(f32), axis=0), axis=0)[0]
    a = _hgrn(proj, proj_c, lb, hgrn_norm_g[0], width_a)
    wr, br = _router_params(router_g_w[0], router_g_b[0], router_e_w[0], router_e_b[0])
    x1, h2, route, cnt = _even_out(x, a, proj, gmlp_ln_g[0], gmlp_ln_b[0], gmlp_w_s[0].astype(bf16), gmlp_b_s[0].T,
                                   even_w_out[0].astype(bf16), gt_m, norm_ffn_g[0], sc_f, sh_f, wr, br)
    sh_m1, sc_m1, gt_m1, sh_f1, sc_f1, gt_f1 = latent_mod(1)
    x, h = _hier_moe(x1, h2, route, cnt, gt_f, norm_mix_g[1], sc_m1, sh_m1, 0, exp_w_gate, exp_w_up, exp_w_down, False)

    pc = _matmul(h.reshape(b * l, d), odd_w_in[0].astype(bf16), 2048, 1024).reshape(b, l, -1)
    wr, br = _router_params(router_g_w[1], router_g_b[1], router_e_w[1], router_e_b[1])
    x1, h2, route, cnt = _odd_out(x, pc, odd_conv_w[0], odd_w_out[0].astype(bf16), gt_m1, norm_ffn_g[1], sc_f1, sh_f1,
                                  wr, br)
    zero = jnp.zeros((b, 1, d), f32)
    return _hier_moe(x1, h2, route, cnt, gt_f1, final_g, zero, zero, 1, exp_w_gate, exp_w_up, exp_w_down, True)
```

```python
import functools

import jax
import jax.numpy as jnp
from jax import lax
from jax.experimental import pallas as pl
from jax.experimental.pallas import tpu as pltpu

f32 = jnp.float32
bf16 = jnp.bfloat16
i32 = jnp.int32

EPS = 1e-6
HEAD_DIM = 128
SCAN_CHUNK = 64
GATE_ROWS = 128
CHUNK_B = 128
N_GROUPS_B = 8
GRID_W = 64
N_EXPERT_GROUPS = 4
EXPERTS_PER_GROUP = 8
N_EXPERTS = N_EXPERT_GROUPS * EXPERTS_PER_GROUP
TOP_K = 2
ROUTE_LANES = 128
MOE_BLOCK = 256
SUB = 256
VMEM_LIMIT = 56 * 1024 * 1024

NT = (((1,), (1,)), ((), ()))
TN = (((0,), (0,)), ((), ()))


def _cparams(n_axes):
    return pltpu.CompilerParams(dimension_semantics=("arbitrary",) * n_axes, vmem_limit_bytes=VMEM_LIMIT)


def _silu(v):
    return v * jax.nn.sigmoid(v)


def _mod_kernel(c_ref, w_ref, b_ref, o_ref):
    s = _silu(c_ref[...])
    o_ref[0] = jnp.dot(s, w_ref[0], preferred_element_type=f32, precision=lax.Precision.HIGHEST) + b_ref[0]


def _mod(c_all, w_mod, b_mod):
    depth, d, n = w_mod.shape
    rows = c_all.shape[0]
    tn = 1024
    return pl.pallas_call(
        _mod_kernel,
        grid=(depth, n // tn),
        in_specs=[
            pl.BlockSpec((rows, d), lambda l, j: (0, 0)),
            pl.BlockSpec((1, d, tn), lambda l, j: (l, 0, j)),
            pl.BlockSpec((1, 1, tn), lambda l, j: (l, 0, j)),
        ],
        out_specs=pl.BlockSpec((1, rows, tn), lambda l, j: (l, 0, j)),
        out_shape=jax.ShapeDtypeStruct((depth, rows, n), f32),
        compiler_params=_cparams(2),
        name="mod",
    )(c_all, w_mod, b_mod.reshape(depth, 1, n))


def _rms_mod(x, g, sc, sh):
    r = lax.rsqrt(jnp.sum(x * x, axis=-1, keepdims=True) * (1.0 / x.shape[-1]) + EPS)
    return (x * r) * (g * (1.0 + sc)) + sh


def _norm_matmul_kernel(x_ref, g_ref, sc_ref, sh_ref, w_ref, o_ref, h_ref):
    @pl.when(pl.program_id(2) == 0)
    def _():
        h_ref[...] = _rms_mod(x_ref[0], g_ref[...], sc_ref[0], sh_ref[0]).astype(bf16)

    o_ref[0] = jnp.dot(h_ref[...], w_ref[...], preferred_element_type=f32).astype(o_ref.dtype)


def _norm_matmul(x, g, sc, sh, w, tm, tn):
    b, l, d = x.shape
    n = w.shape[1]
    return pl.pallas_call(
        _norm_matmul_kernel,
        grid=(b, l // tm, n // tn),
        in_specs=[
            pl.BlockSpec((1, tm, d), lambda bi, i, j: (bi, i, 0)),
            pl.BlockSpec((1, d), lambda bi, i, j: (0, 0)),
            pl.BlockSpec((1, 1, d), lambda bi, i, j: (bi, 0, 0)),
            pl.BlockSpec((1, 1, d), lambda bi, i, j: (bi, 0, 0)),
            pl.BlockSpec((d, tn), lambda bi, i, j: (0, j)),
        ],
        out_specs=pl.BlockSpec((1, tm, tn), lambda bi, i, j: (bi, i, j)),
        out_shape=jax.ShapeDtypeStruct((b, l, n), bf16),
        scratch_shapes=[pltpu.VMEM((tm, d), bf16)],
        compiler_params=_cparams(3),
        name="norm_matmul",
    )(x, g.reshape(1, d), sc, sh, w)


def _matmul_kernel(x_ref, w_ref, o_ref):
    o_ref[...] = jnp.dot(x_ref[...], w_ref[...], preferred_element_type=f32).astype(o_ref.dtype)


def _matmul(x, w, tm, tn):
    m, d = x.shape
    n = w.shape[1]
    tm = min(tm, m)
    return pl.pallas_call(
        _matmul_kernel,
        grid=(m // tm, n // tn),
        in_specs=[pl.BlockSpec((tm, d), lambda i, j: (i, 0)), pl.BlockSpec((d, tn), lambda i, j: (0, j))],
        out_specs=pl.BlockSpec((tm, tn), lambda i, j: (i, j)),
        out_shape=jax.ShapeDtypeStruct((m, n), bf16),
        compiler_params=_cparams(2),
        name="matmul",
    )(x, w)


def _hgrn_kernel(q_ref, ff_ref, fb_ref, v_ref, g_ref, cff_ref, cfb_ref, cv_ref, lb_ref, ng_ref,
                 a_ref, of_ref, ob_ref, qd_ref, kd_ref, ks_ref, dec_ref, st_ref):
    C = SCAN_CHUNK
    G = GATE_ROWS
    L = q_ref.shape[1]
    Lc = cff_ref.shape[1]
    n = L // C
    row = lax.broadcasted_iota(i32, (C, C), 0)
    col = lax.broadcasted_iota(i32, (C, C), 1)
    lower = col <= row
    upper = col >= row
    grow = lax.broadcasted_iota(i32, (G, G), 0)
    gcol = lax.broadcasted_iota(i32, (G, G), 1)
    same_chunk = (grow // C) == (gcol // C)
    tri = ((same_chunk & (gcol <= grow)).astype(bf16), (same_chunk & (gcol >= grow)).astype(bf16))
    lbs = (lb_ref[0:1, :], lb_ref[1:2, :])

    def gate_block(f_raw, d):
        f = lbs[d] + (1.0 - lbs[d]) * jax.nn.sigmoid(f_raw.astype(f32))
        logf = jnp.log(f)
        hi = logf.astype(bf16)
        low = (logf - hi.astype(f32)).astype(bf16)
        cum = jnp.dot(tri[d], hi, preferred_element_type=f32) + jnp.dot(tri[d], low, preferred_element_type=f32)
        k3 = (1.0 - f).reshape(G // C, C, HEAD_DIM)
        cum3 = cum.reshape(G // C, C, HEAD_DIM)
        edge3 = cum3[:, C - 1:C, :] if d == 0 else cum3[:, 0:1, :]
        k_dec = (k3 * jnp.exp(-cum3)).reshape(G, HEAD_DIM).astype(bf16)
        k_st = (k3 * jnp.exp(edge3 - cum3)).reshape(G, HEAD_DIM).astype(bf16)
        dec = [jnp.broadcast_to(jnp.exp(edge3[c]), (HEAD_DIM, HEAD_DIM)).T for c in range(G // C)]
        return cum, k_dec, k_st, dec

    f_refs = (cff_ref, cfb_ref)
    for d in range(2):
        st = jnp.zeros((HEAD_DIM, HEAD_DIM), f32)
        blocks = range(Lc // G) if d == 0 else range(Lc // G - 1, -1, -1)
        for gb in blocks:
            _, _, k_st, dec = gate_block(f_refs[d][0, gb * G:(gb + 1) * G, :], d)
            chunks = range(G // C) if d == 0 else range(G // C - 1, -1, -1)
            for c in chunks:
                v = cv_ref[0, gb * G + c * C:gb * G + (c + 1) * C, :]
                st = (dec[c] * st
                      + lax.dot_general(k_st[c * C:(c + 1) * C, :], v, TN, preferred_element_type=f32))
        st_ref[d] = st

    def precompute(i, carry):
        off = pl.multiple_of(i * G, G)
        rows = pl.ds(off, G)
        qs = _silu(q_ref[0, rows, :].astype(f32))
        for d, f_ref in enumerate((ff_ref, fb_ref)):
            cum, k_dec, k_st, dec = gate_block(f_ref[0, rows, :], d)
            qd_ref[d, rows, :] = (qs * jnp.exp(cum)).astype(bf16)
            kd_ref[d, rows, :] = k_dec
            ks_ref[d, rows, :] = k_st
            for c in range(G // C):
                dec_ref[d, i * (G // C) + c] = dec[c]
        return carry

    lax.fori_loop(0, L // G, precompute, 0, unroll=4)

    def chunk_step(ci, d, mask, o_ref):
        rows = pl.ds(pl.multiple_of(ci * C, C), C)
        qd = qd_ref[d, rows, :]
        v = v_ref[0, rows, :]
        s = lax.dot_general(qd, kd_ref[d, rows, :], NT, preferred_element_type=f32)
        s = jnp.where(mask, s, 0.0).astype(bf16)
        st = st_ref[d]
        o_ref[rows, :] = (jnp.dot(s, v, preferred_element_type=f32)
                          + jnp.dot(qd, st.astype(bf16), preferred_element_type=f32))
        st_ref[d] = dec_ref[d, ci] * st + lax.dot_general(ks_ref[d, rows, :], v, TN, preferred_element_type=f32)

    def body(c, carry):
        chunk_step(c, 0, lower, of_ref)
        chunk_step(n - 1 - c, 1, upper, ob_ref)
        return carry

    lax.fori_loop(0, n, body, 0, unroll=8)

    R = 512 if L % 512 == 0 else C

    def epilogue(r, carry):
        rows = pl.ds(pl.multiple_of(r * R, R), R)
        o = of_ref[rows, :] + ob_ref[rows, :]
        o = o * lax.rsqrt(jnp.mean(o * o, axis=-1, keepdims=True) + EPS)
        o = o * ng_ref[...]
        a_ref[0, rows, :] = (o * _silu(g_ref[0, rows, :].astype(f32))).astype(a_ref.dtype)
        return carry

    lax.fori_loop(0, L // R, epilogue, 0)


def _hgrn(proj, proj_c, lb, norm_g, width_a):
    b, l, _ = proj.shape
    lc = proj_c.shape[1]
    nh = width_a // HEAD_DIM

    def seg(k):
        return pl.BlockSpec((1, l, HEAD_DIM), lambda bi, h, k=k: (bi, 0, k * nh + h))

    def cseg(k):
        return pl.BlockSpec((1, lc, HEAD_DIM), lambda bi, h, k=k: (bi, 0, k * nh + h))

    return pl.pallas_call(
        _hgrn_kernel,
        grid=(b, nh),
        in_specs=[seg(0), seg(1), seg(2), seg(3), seg(4), cseg(0), cseg(1), cseg(2),
                  pl.BlockSpec((2, HEAD_DIM), lambda bi, h: (0, h)),
                  pl.BlockSpec((1, HEAD_DIM), lambda bi, h: (0, h))],
        out_specs=pl.BlockSpec((1, l, HEAD_DIM), lambda bi, h: (bi, 0, h)),
        out_shape=jax.ShapeDtypeStruct((b, l, width_a), bf16),
        scratch_shapes=[pltpu.VMEM((l, HEAD_DIM), f32), pltpu.VMEM((l, HEAD_DIM), f32),
                        pltpu.VMEM((2, l, HEAD_DIM), bf16), pltpu.VMEM((2, l, HEAD_DIM), bf16),
                        pltpu.VMEM((2, l, HEAD_DIM), bf16),
                        pltpu.VMEM((2, l // SCAN_CHUNK, HEAD_DIM, HEAD_DIM), f32),
                        pltpu.VMEM((2, HEAD_DIM, HEAD_DIM), f32)],
        compiler_params=_cparams(2),
        name="hgrn",
    )(proj, proj, proj, proj, proj, proj_c, proj_c, proj_c, lb, norm_g.reshape(1, width_a))


def _residual_norm_route(rows, valid, x, y, gt, g2, sc, sh, wr_ref, br_ref, x1_ref, h2_ref, route_ref, cnt_ref,
                         cnt_acc):
    x1 = x + gt * y
    x1_ref[0, rows, :] = x1
    h2 = _rms_mod(x1, g2, sc, sh)
    h2_ref[0, rows, :] = h2
    logits = jnp.dot(h2.astype(bf16), wr_ref[...], preferred_element_type=f32) + br_ref[...]
    tm = logits.shape[0]
    lane = lax.broadcasted_iota(i32, (tm, ROUTE_LANES), 1)
    lane_f = lane.astype(f32)
    neg = -jnp.inf
    gl = jnp.where(lane < N_EXPERT_GROUPS, logits, neg)
    gmax = jnp.max(gl, axis=-1, keepdims=True)
    p_top = 1.0 / jnp.sum(jnp.exp(gl - gmax), axis=-1, keepdims=True)
    grp = jnp.min(jnp.where(gl == gmax, lane_f, float(ROUTE_LANES)), axis=-1, keepdims=True)
    lo = float(N_EXPERT_GROUPS) + grp * float(EXPERTS_PER_GROUP)
    emask = (lane_f >= lo) & (lane_f < lo + float(EXPERTS_PER_GROUP))
    el = jnp.where(emask, logits, neg)
    emax = jnp.max(el, axis=-1, keepdims=True)
    esum = jnp.sum(jnp.exp(el - emax), axis=-1, keepdims=True)
    i1 = jnp.min(jnp.where(el == emax, lane_f, float(ROUTE_LANES)), axis=-1, keepdims=True)
    p1 = 1.0 / esum
    el2 = jnp.where(lane_f == i1, neg, el)
    emax2 = jnp.max(el2, axis=-1, keepdims=True)
    i2 = jnp.min(jnp.where(el2 == emax2, lane_f, float(ROUTE_LANES)), axis=-1, keepdims=True)
    p2 = jnp.exp(emax2 - emax) / esum
    den = p1 + p2
    w1 = p_top * p1 / den
    w2 = p_top * p2 / den
    e1 = i1 - float(N_EXPERT_GROUPS)
    e2 = i2 - float(N_EXPERT_GROUPS)
    hot1 = lane_f == i1
    hot2 = lane_f == i2
    both = jnp.where((hot1 | hot2) & valid, 1.0, 0.0)
    trow = lax.broadcasted_iota(i32, (tm, tm), 0)
    tcol = lax.broadcasted_iota(i32, (tm, tm), 1)
    before = (tcol < trow).astype(bf16)
    base = jnp.dot(before, both.astype(bf16), preferred_element_type=f32) + cnt_acc[...]
    r1 = jnp.sum(jnp.where(hot1, base, 0.0), axis=-1, keepdims=True)
    r2 = jnp.sum(jnp.where(hot2, base, 0.0), axis=-1, keepdims=True)
    cnt = cnt_acc[...] + jnp.sum(both, axis=0, keepdims=True)
    cnt_acc[...] = cnt
    cnt_ref[...] = cnt
    route = jnp.zeros((tm, ROUTE_LANES), f32)
    for k, val in enumerate((e1, e2, w1, w2, r1, r2)):
        route = jnp.where(lane == k, val, route)
    route_ref[0, rows, :] = route


def _out_pipeline(stage_fn, x_ref, wo_ref, gt_ref, g2_ref, sc_ref, sh_ref, wr_ref, br_ref,
                  x1_ref, h2_ref, route_ref, cnt_ref, t_bufs, y_bufs, cnt_acc):
    p = pl.program_id(0)

    @pl.when(p == 0)
    def _():
        for ref in (*t_bufs, *y_bufs, cnt_acc):
            ref[...] = jnp.zeros_like(ref)

    for half in range(2):
        rows = slice(half * SUB, (half + 1) * SUB)
        _residual_norm_route(rows, p > 0, x_ref[0, rows, :], y_bufs[half][...], gt_ref[0], g2_ref[...], sc_ref[0],
                             sh_ref[0], wr_ref, br_ref, x1_ref, h2_ref, route_ref, cnt_ref, cnt_acc)
        y_bufs[1 - half][...] = jnp.dot(t_bufs[1 - half][...], wo_ref[...], preferred_element_type=f32)
        stage_fn(rows, t_bufs[half])


def _gelu_tanh(v):
    return 0.5 * v * (1.0 + jnp.tanh(0.7978845608028654 * (v + 0.044715 * (v * v * v))))


def _even_out_kernel(a_ref, u_ref, v_ref, lng_ref, lnb_ref, ws_ref, bs_ref, x_ref, wo_ref, gt_ref, g2_ref, sc_ref,
                     sh_ref, wr_ref, br_ref, x1_ref, h2_ref, route_ref, cnt_ref, t0, t1, y0, y1, cnt_acc):
    wa = a_ref.shape[2]
    gd = u_ref.shape[2] // N_GROUPS_B

    def stage(rows, t_ref):
        t_ref[:, 0:wa] = a_ref[0, rows, :]
        gv = _gelu_tanh(v_ref[0, rows, :].astype(f32))
        mu = jnp.mean(gv, axis=-1, keepdims=True)
        dv = gv - mu
        var = jnp.mean(dv * dv, axis=-1, keepdims=True)
        vn = (dv * lax.rsqrt(var + EPS) * lng_ref[...] + lnb_ref[...]).astype(bf16)
        for nb in range(SUB // CHUNK_B):
            rs = slice(nb * CHUNK_B, (nb + 1) * CHUNK_B)
            ru = slice(rows.start + nb * CHUNK_B, rows.start + (nb + 1) * CHUNK_B)
            for g in range(N_GROUPS_B):
                cs = slice(g * gd, (g + 1) * gd)
                s = jnp.dot(ws_ref[g], vn[rs, cs], preferred_element_type=f32) + bs_ref[:, g:g + 1]
                t_ref[rs, wa + g * gd:wa + (g + 1) * gd] = (_gelu_tanh(u_ref[0, ru, cs].astype(f32)) * s).astype(bf16)

    _out_pipeline(stage, x_ref, wo_ref, gt_ref, g2_ref, sc_ref, sh_ref, wr_ref, br_ref,
                  x1_ref, h2_ref, route_ref, cnt_ref, (t0, t1), (y0, y1), cnt_acc)


def _odd_out_kernel(hx_ref, bg_ref, cg_ref, cw_ref, x_ref, wo_ref, gt_ref, g2_ref, sc_ref,
                    sh_ref, wr_ref, br_ref, x1_ref, h2_ref, route_ref, cnt_ref, t0, t1, y0, y1, cnt_acc):
    def stage(rows, t_ref):
        z = cg_ref[0, rows, :].astype(f32) * hx_ref[0, rows, :].astype(f32)
        pos = lax.broadcasted_iota(i32, (SUB, 1), 0) % GRID_W
        z_prev = jnp.where(pos == 0, 0.0, pltpu.roll(z, 1, 0))
        z_next = jnp.where(pos == GRID_W - 1, 0.0, pltpu.roll(z, SUB - 1, 0))
        conv = z_prev * cw_ref[0:1, :] + z * cw_ref[1:2, :] + z_next * cw_ref[2:3, :]
        t_ref[...] = (bg_ref[0, rows, :].astype(f32) * conv).astype(bf16)

    _out_pipeline(stage, x_ref, wo_ref, gt_ref, g2_ref, sc_ref, sh_ref, wr_ref, br_ref,
                  x1_ref, h2_ref, route_ref, cnt_ref, (t0, t1), (y0, y1), cnt_acc)


def _mixer_out(kernel_fn, name, stage_args, stage_specs, x, w_out, gt, g2, sc, sh, wr, br):
    b, l, d = x.shape
    blk = 2 * SUB
    per_b = l // blk
    n = b * per_b
    k = w_out.shape[0]

    def ahead(p):
        q = jnp.minimum(p, n - 1)
        return q // per_b, q % per_b

    def behind(p):
        q = jnp.maximum(p - 1, 0)
        return q // per_b, q % per_b

    const = lambda shape: pl.BlockSpec(shape, lambda p: (0,) * len(shape))
    per_b_vec = pl.BlockSpec((1, 1, d), lambda p: (behind(p)[0], 0, 0))
    tile = lambda width: pl.BlockSpec((1, blk, width), lambda p: (*behind(p), 0))
    return pl.pallas_call(
        kernel_fn,
        grid=(n + 1,),
        in_specs=stage_specs(ahead, blk, const) + [
            tile(d), const((k, d)), per_b_vec, const((1, d)), per_b_vec, per_b_vec,
            const((d, ROUTE_LANES)), const((1, ROUTE_LANES))],
        out_specs=[tile(d), tile(d), tile(ROUTE_LANES), const((1, ROUTE_LANES))],
        out_shape=[jax.ShapeDtypeStruct((b, l, d), f32), jax.ShapeDtypeStruct((b, l, d), f32),
                   jax.ShapeDtypeStruct((b, l, ROUTE_LANES), f32), jax.ShapeDtypeStruct((1, ROUTE_LANES), f32)],
        scratch_shapes=[pltpu.VMEM((SUB, k), bf16), pltpu.VMEM((SUB, k), bf16),
                        pltpu.VMEM((SUB, d), f32), pltpu.VMEM((SUB, d), f32), pltpu.VMEM((1, ROUTE_LANES), f32)],
        compiler_params=_cparams(1),
        name=name,
    )(*stage_args, x, w_out, gt, g2.reshape(1, d), sc, sh, wr, br)


def _even_out(x, a, proj, ln_g, ln_b, w_s, b_s_t, w_out, gt, g2, sc, sh, wr, br):
    wa = a.shape[2]
    wb = ln_g.shape[0]
    ub = (5 * wa) // wb

    def stage_specs(ahead, blk, const):
        return [pl.BlockSpec((1, blk, wa), lambda p: (*ahead(p), 0)),
                pl.BlockSpec((1, blk, wb), lambda p: (*ahead(p), ub)),
                pl.BlockSpec((1, blk, wb), lambda p: (*ahead(p), ub + 1)),
                const((1, wb)), const((1, wb)), const((N_GROUPS_B, CHUNK_B, CHUNK_B)), const((CHUNK_B, N_GROUPS_B))]

    return _mixer_out(_even_out_kernel, "even_out",
                      (a, proj, proj, ln_g.reshape(1, wb), ln_b.reshape(1, wb), w_s, b_s_t), stage_specs,
                      x, w_out, gt, g2, sc, sh, wr, br)


def _odd_out(x, pc, conv_w, w_out, gt, g2, sc, sh, wr, br):
    wc = w_out.shape[0]

    def stage_specs(ahead, blk, const):
        return [pl.BlockSpec((1, blk, wc), lambda p, j=j: (*ahead(p), j)) for j in range(3)] + [const(conv_w.shape)]

    return _mixer_out(_odd_out_kernel, "odd_out", (pc, pc, pc, conv_w), stage_specs,
                      x, w_out, gt, g2, sc, sh, wr, br)


def _row_gather_start(idx_ref, n_rows, src_hbm, dst_buf, sem):
    def body(r, carry):
        t = idx_ref[0, 0, r]
        pltpu.make_async_copy(src_hbm.at[pl.ds(t, 1), :], dst_buf.at[pl.ds(r, 1), :], sem).start()
        return carry

    lax.fori_loop(0, n_rows, body, 0, unroll=8)


def _row_gather_wait(n_rows, src_hbm, dst_buf, sem):
    def body(r, carry):
        pltpu.make_async_copy(src_hbm.at[pl.ds(0, 1), :], dst_buf.at[pl.ds(r, 1), :], sem).wait()
        return carry

    lax.fori_loop(0, n_rows, body, 0, unroll=8)


def _dispatch_kernel(pos_ref, h_ref, xs_hbm, sem):
    tm = h_ref.shape[0]

    def start(r, carry):
        for k in range(TOP_K):
            pltpu.make_async_copy(h_ref.at[pl.ds(r, 1), :], xs_hbm.at[pl.ds(pos_ref[0, 0, k * tm + r], 1), :],
                                  sem.at[0]).start()
        return carry

    lax.fori_loop(0, tm, start, 0, unroll=8)

    def wait(r, carry):
        pltpu.make_async_copy(h_ref.at[pl.ds(0, 1), :], xs_hbm.at[pl.ds(0, 1), :], sem.at[0]).wait()
        return carry

    lax.fori_loop(0, TOP_K * tm, wait, 0, unroll=8)


def _pos_blocks(pos, tm):
    nt = pos.shape[0] // tm
    return pos.reshape(nt, tm, TOP_K).transpose(0, 2, 1).reshape(nt, 1, TOP_K * tm)


def _moe_dispatch(h2, pos, tm):
    t, d = h2.shape
    return pl.pallas_call(
        _dispatch_kernel,
        grid=(t // tm,),
        in_specs=[pl.BlockSpec((1, 1, TOP_K * tm), lambda i: (i, 0, 0), memory_space=pltpu.SMEM),
                  pl.BlockSpec((tm, d), lambda i: (i, 0))],
        out_specs=pl.BlockSpec(memory_space=pl.ANY),
        out_shape=jax.ShapeDtypeStruct((t * TOP_K, d), h2.dtype),
        scratch_shapes=[pltpu.SemaphoreType.DMA((1,))],
        compiler_params=_cparams(1),
        name="moe_dispatch",
    )(_pos_blocks(pos, tm), h2)


def _ffn_kernel(blk_ref, e_ref, lo_ref, hi_ref, first_ref, nxt_ref, x_ref, wg_hbm, wu_hbm, wd_hbm, y_ref,
                wg_f, wu_f, wd_f, wg_s, wu_s, wd_s, sem, *, layer):
    w = pl.program_id(0)
    lo = lo_ref[w]
    hi = hi_ref[w]
    prev = jnp.maximum(w - 1, 0)

    def weight_copies(e):
        return (pltpu.make_async_copy(wg_hbm.at[layer, e], wg_f, sem.at[0]),
                pltpu.make_async_copy(wu_hbm.at[layer, e], wu_f, sem.at[1]),
                pltpu.make_async_copy(wd_hbm.at[layer, e], wd_f, sem.at[2]))

    @pl.when(hi > lo)
    def _():
        @pl.when(first_ref[w] == 1)
        def _():
            @pl.when(w == 0)
            def _():
                for cp in weight_copies(e_ref[0]):
                    cp.start()

            for cp in weight_copies(e_ref[w]):
                cp.wait()
            wg_s[...] = wg_f[...].astype(bf16)
            wu_s[...] = wu_f[...].astype(bf16)
            wd_s[...] = wd_f[...].astype(bf16)

            @pl.when(nxt_ref[w] >= 0)
            def _():
                for cp in weight_copies(nxt_ref[w]):
                    cp.start()

        xb = x_ref[...].astype(bf16)
        gate = jnp.dot(xb, wg_s[...], preferred_element_type=f32)
        up = jnp.dot(xb, wu_s[...], preferred_element_type=f32)
        mid = (_silu(gate) * up).astype(bf16)
        y = jnp.dot(mid, wd_s[...], preferred_element_type=f32)
        r = lax.broadcasted_iota(i32, (y.shape[0], 1), 0)
        mine = (r >= lo) & (r < hi)
        first_visit = (w == 0) | (blk_ref[w] != blk_ref[prev])

        @pl.when(first_visit)
        def _():
            y_ref[...] = jnp.where(mine, y, 0.0)

        @pl.when(jnp.logical_not(first_visit))
        def _():
            y_ref[...] = jnp.where(mine, y, y_ref[...])


def _moe_ffn(xs, items, layer, w_gate, w_up, w_down):
    rows, d = xs.shape
    bm = MOE_BLOCK
    ff = w_gate.shape[3]
    n_items = items[0].shape[0]
    grid_spec = pltpu.PrefetchScalarGridSpec(
        num_scalar_prefetch=6,
        grid=(n_items,),
        in_specs=[
            pl.BlockSpec((bm, d), lambda w, blk, *_: (blk[w], 0)),
            pl.BlockSpec(memory_space=pl.ANY),
            pl.BlockSpec(memory_space=pl.ANY),
            pl.BlockSpec(memory_space=pl.ANY),
        ],
        out_specs=pl.BlockSpec((bm, d), lambda w, blk, *_: (blk[w], 0)),
        scratch_shapes=[pltpu.VMEM((d, ff), f32), pltpu.VMEM((d, ff), f32), pltpu.VMEM((ff, d), f32),
                        pltpu.VMEM((d, ff), bf16), pltpu.VMEM((d, ff), bf16), pltpu.VMEM((ff, d), bf16),
                        pltpu.SemaphoreType.DMA((3,))],
    )
    return pl.pallas_call(
        functools.partial(_ffn_kernel, layer=layer),
        grid_spec=grid_spec,
        out_shape=jax.ShapeDtypeStruct((rows, d), f32),
        compiler_params=_cparams(1),
        name="moe_ffn",
    )(*items, xs, w_gate, w_up, w_down)


def _combine_kernel(pos_cur_ref, pos_nxt_ref, x_ref, gt_ref, ng_ref, nsc_ref, nsh_ref, route_ref, ys_hbm, *rest, final):
    out_refs, (ybuf, sem) = rest[:-2], rest[-2:]
    i = pl.program_id(0)
    n = pl.num_programs(0)
    tm = x_ref.shape[1]
    slot = lax.rem(i, 2)

    @pl.when(i == 0)
    def _():
        _row_gather_start(pos_cur_ref, TOP_K * tm, ys_hbm, ybuf.at[0], sem.at[0])

    @pl.when(i + 1 < n)
    def _():
        _row_gather_start(pos_nxt_ref, TOP_K * tm, ys_hbm, ybuf.at[1 - slot], sem.at[1 - slot])

    _row_gather_wait(TOP_K * tm, ys_hbm, ybuf.at[slot], sem.at[slot])
    route = route_ref[0]
    f = ybuf[slot, 0:tm, :] * route[:, TOP_K:TOP_K + 1] + ybuf[slot, tm:2 * tm, :] * route[:, TOP_K + 1:TOP_K + 2]
    xo = x_ref[0] + gt_ref[0] * f
    hn = _rms_mod(xo, ng_ref[...], nsc_ref[0], nsh_ref[0])
    if final:
        out_refs[0][0] = hn
    else:
        out_refs[0][0] = xo
        out_refs[1][0] = hn.astype(out_refs[1].dtype)


def _moe_combine(x1, gt, norm_g, norm_sc, norm_sh, route, ys, pos, tm, final):
    b, l, d = x1.shape
    nt = (b * l) // tm
    per_b = l // tm
    pos3 = _pos_blocks(pos, tm)
    tile = pl.BlockSpec((1, tm, d), lambda i: (i // per_b, i % per_b, 0))
    per_b_vec = pl.BlockSpec((1, 1, d), lambda i: (i // per_b, 0, 0))
    if final:
        out_specs, out_shape = tile, jax.ShapeDtypeStruct((b, l, d), f32)
    else:
        out_specs = [tile, tile]
        out_shape = [jax.ShapeDtypeStruct((b, l, d), f32), jax.ShapeDtypeStruct((b, l, d), bf16)]
    return pl.pallas_call(
        functools.partial(_combine_kernel, final=final),
        grid=(nt,),
        in_specs=[
            pl.BlockSpec((1, 1, TOP_K * tm), lambda i: (i, 0, 0), memory_space=pltpu.SMEM),
            pl.BlockSpec((1, 1, TOP_K * tm), lambda i: (jnp.minimum(i + 1, nt - 1), 0, 0), memory_space=pltpu.SMEM),
            tile, per_b_vec,
            pl.BlockSpec((1, d), lambda i: (0, 0)),
            per_b_vec, per_b_vec,
            pl.BlockSpec((1, tm, ROUTE_LANES), lambda i: (i // per_b, i % per_b, 0)),
            pl.BlockSpec(memory_space=pl.ANY),
        ],
        out_specs=out_specs,
        out_shape=out_shape,
        scratch_shapes=[pltpu.VMEM((2, TOP_K * tm, d), f32), pltpu.SemaphoreType.DMA((2,))],
        compiler_params=_cparams(1),
        name="moe_combine",
    )(pos3, pos3, x1, gt, norm_g.reshape(1, d), norm_sc, norm_sh, route, ys)


def _dispatch_plan(route, cnt, n_tok):
    bm = MOE_BLOCK
    n_rows = n_tok * TOP_K
    n_blk = n_rows // bm
    experts = jnp.arange(N_EXPERTS, dtype=i32)
    counts = cnt[0, N_EXPERT_GROUPS:N_EXPERT_GROUPS + N_EXPERTS].astype(i32)
    ends = jnp.cumsum(counts)
    starts = ends - counts
    eid = route[..., 0:TOP_K].astype(i32).reshape(n_tok, TOP_K)
    rank = route[..., 2 * TOP_K:3 * TOP_K].astype(i32).reshape(n_tok, TOP_K)
    pos = rank + jnp.sum(jnp.where(eid[..., None] == experts, starts, 0), axis=-1)
    first_blk = starts // bm
    n_it = jnp.where(counts > 0, (ends - 1) // bm - first_blk + 1, 0)
    it_end = jnp.cumsum(n_it)
    it_start = it_end - n_it
    w = jnp.arange(n_blk + N_EXPERTS, dtype=i32)
    live = w < it_end[-1]
    e_last = jnp.max(jnp.where(counts > 0, experts, 0))
    e_w = jnp.where(live, jnp.sum((it_end[None, :] <= w[:, None]).astype(i32), axis=1), e_last)
    hot = e_w[:, None] == experts
    pick = lambda tab: jnp.sum(jnp.where(hot, tab, 0), axis=1)
    blk_w = jnp.where(live, pick(first_blk) + w - pick(it_start), n_blk - 1)
    lo_w = jnp.where(live, jnp.maximum(pick(starts), blk_w * bm) - blk_w * bm, 0)
    hi_w = jnp.where(live, jnp.minimum(pick(ends), (blk_w + 1) * bm) - blk_w * bm, 0)
    first_w = (live & ((w == 0) | (e_w != jnp.roll(e_w, 1)))).astype(i32)
    later = (experts[None, :] > experts[:, None]) & (counts[None, :] > 0)
    nxt_tab = jnp.min(jnp.where(later, experts[None, :], N_EXPERTS), axis=1)
    nxt_w = jnp.where(live, pick(jnp.where(nxt_tab < N_EXPERTS, nxt_tab, -1)), -1)
    return pos, (blk_w, e_w, lo_w, hi_w, first_w, nxt_w)


def _hier_moe(x1, h2, route, cnt, gt, norm_g, norm_sc, norm_sh, layer, w_gate, w_up, w_down, final):
    b, l, d = x1.shape
    n_tok = b * l
    assert (n_tok * TOP_K) % MOE_BLOCK == 0
    pos, items = _dispatch_plan(route, cnt, n_tok)
    xs = _moe_dispatch(h2.reshape(n_tok, d), pos, 512)
    ys = _moe_ffn(xs, items, layer, w_gate, w_up, w_down)
    return _moe_combine(x1, gt, norm_g, norm_sc, norm_sh, route, ys, pos, 256, final)


def _router_params(rg_w, rg_b, re_w, re_b):
    d = rg_w.shape[0]
    pad = ROUTE_LANES - N_EXPERT_GROUPS - N_EXPERTS
    wr = jnp.concatenate([rg_w, re_w, jnp.zeros((d, pad), f32)], axis=1).astype(bf16)
    br = jnp.concatenate([rg_b, re_b, jnp.zeros((pad,), f32)]).reshape(1, ROUTE_LANES)
    return wr, br


def kernel(x, c, ctx, c_ctx, w_mod, b_mod, norm_mix_g, norm_ffn_g, final_g, even_w_in, even_w_out, hgrn_lb_raw,
           hgrn_norm_g, gmlp_ln_g, gmlp_ln_b, gmlp_w_s, gmlp_b_s, odd_w_in, odd_conv_w, odd_w_out, router_g_w,
           router_g_b, router_e_w, router_e_b, exp_w_gate, exp_w_up, exp_w_down):
    b, l, d = x.shape
    depth = w_mod.shape[0]
    assert depth == 2, "layer plan below is written for one even and one odd layer"
    width_a = hgrn_norm_g.shape[1]
    lc = ctx.shape[1]

    rows = -(-(b + 1) // 8) * 8
    c_all = jnp.zeros((rows, d), f32).at[:b].set(c).at[b].set(c_ctx)
    mod = _mod(c_all, w_mod, b_mod)

    def latent_mod(layer):
        return [m.reshape(b, 1, d) for m in jnp.split(mod[layer, :b], 6, axis=-1)]

    sh_m, sc_m, gt_m, sh_f, sc_f, gt_f = latent_mod(0)
    csh_m = jnp.broadcast_to(mod[0, b, 0:d].reshape(1, 1, d), (b, 1, d))
    csc_m = jnp.broadcast_to(mod[0, b, d:2 * d].reshape(1, 1, d), (b, 1, d))
    w_in = even_w_in[0].astype(bf16)
    proj = _norm_matmul(x, norm_mix_g[0], sc_m, sh_m, w_in, min(l, 1024), 1792)
    proj_c = _norm_matmul(ctx, norm_mix_g[0], csc_m, csh_m, w_in[:, width_a:4 * width_a], lc, 1024)
    lb = jnp.cumsum(jax.nn.softmax(hgrn_lb_raw.astype(f32), axis=0), axis=0)[0]
    a = _hgrn(proj, proj_c, lb, hgrn_norm_g[0], width_a)
    wr, br = _router_params(router_g_w[0], router_g_b[0], router_e_w[0], router_e_b[0])
    x1, h2, route, cnt = _even_out(x, a, proj, gmlp_ln_g[0], gmlp_ln_b[0], gmlp_w_s[0].astype(bf16), gmlp_b_s[0].T,
                                   even_w_out[0].astype(bf16), gt_m, norm_ffn_g[0], sc_f, sh_f, wr, br)
    sh_m1, sc_m1, gt_m1, sh_f1, sc_f1, gt_f1 = latent_mod(1)
    x, h = _hier_moe(x1, h2, route, cnt, gt_f, norm_mix_g[1], sc_m1, sh_m1, 0, exp_w_gate, exp_w_up, exp_w_down, False)

    pc = _matmul(h.reshape(b * l, d), odd_w_in[0].astype(bf16), 2048, 1024).reshape(b, l, -1)
    wr, br = _router_params(router_g_w[1], router_g_b[1], router_e_w[1], router_e_b[1])
    x1, h2, route, cnt = _odd_out(x, pc, odd_conv_w[0], odd_w_out[0].astype(bf16), gt_m1, norm_ffn_g[1], sc_f1, sh_f1,
                                  wr, br)
    zero = jnp.zeros((b, 1, d), f32)
    return _hier_moe(x1, h2, route, cnt, gt_f1, final_g, zero, zero, 1, exp_w_gate, exp_w_up, exp_w_down, True)
```

```python
import functools

import jax
import jax.numpy as jnp
from jax import lax
from jax.experimental import pallas as pl
from jax.experimental.pallas import tpu as pltpu

f32 = jnp.float32
bf16 = jnp.bfloat16
i32 = jnp.int32

EPS = 1e-6
HEAD_DIM = 128
SCAN_CHUNK = 64
GATE_ROWS = 128
CHUNK_B = 128
N_GROUPS_B = 8
GRID_W = 64
N_EXPERT_GROUPS = 4
EXPERTS_PER_GROUP = 8
N_EXPERTS = N_EXPERT_GROUPS * EXPERTS_PER_GROUP
TOP_K = 2
ROUTE_LANES = 128
MOE_BLOCK = 256
SUB = 256
VMEM_LIMIT = 56 * 1024 * 1024

NT = (((1,), (1,)), ((), ()))
TN = (((0,), (0,)), ((), ()))


def _cparams(n_axes):
    return pltpu.CompilerParams(dimension_semantics=("arbitrary",) * n_axes, vmem_limit_bytes=VMEM_LIMIT)


def _silu(v):
    return v * jax.nn.sigmoid(v)


def _mod_kernel(c_ref, w_ref, b_ref, o_ref):
    s = _silu(c_ref[...])
    o_ref[0] = jnp.dot(s, w_ref[0], preferred_element_type=f32, precision=lax.Precision.HIGHEST) + b_ref[0]


def _mod(c_all, w_mod, b_mod):
    depth, d, n = w_mod.shape
    rows = c_all.shape[0]
    tn = 2048
    return pl.pallas_call(
        _mod_kernel,
        grid=(depth, n // tn),
        in_specs=[
            pl.BlockSpec((rows, d), lambda l, j: (0, 0)),
            pl.BlockSpec((1, d, tn), lambda l, j: (l, 0, j)),
            pl.BlockSpec((1, 1, tn), lambda l, j: (l, 0, j)),
        ],
        out_specs=pl.BlockSpec((1, rows, tn), lambda l, j: (l, 0, j)),
        out_shape=jax.ShapeDtypeStruct((depth, rows, n), f32),
        compiler_params=_cparams(2),
        name="mod",
    )(c_all, w_mod, b_mod.reshape(depth, 1, n))


def _rms_mod(x, g, sc, sh):
    r = lax.rsqrt(jnp.sum(x * x, axis=-1, keepdims=True) * (1.0 / x.shape[-1]) + EPS)
    return (x * r) * (g * (1.0 + sc)) + sh


def _norm_matmul_kernel(x_ref, g_ref, sc_ref, sh_ref, w_ref, o_ref, h_ref):
    @pl.when(pl.program_id(2) == 0)
    def _():
        h_ref[...] = _rms_mod(x_ref[0], g_ref[...], sc_ref[0], sh_ref[0]).astype(bf16)

    o_ref[0] = jnp.dot(h_ref[...], w_ref[...], preferred_element_type=f32).astype(o_ref.dtype)


def _norm_matmul(x, g, sc, sh, w, tm, tn):
    b, l, d = x.shape
    n = w.shape[1]
    return pl.pallas_call(
        _norm_matmul_kernel,
        grid=(b, l // tm, n // tn),
        in_specs=[
            pl.BlockSpec((1, tm, d), lambda bi, i, j: (bi, i, 0)),
            pl.BlockSpec((1, d), lambda bi, i, j: (0, 0)),
            pl.BlockSpec((1, 1, d), lambda bi, i, j: (bi, 0, 0)),
            pl.BlockSpec((1, 1, d), lambda bi, i, j: (bi, 0, 0)),
            pl.BlockSpec((d, tn), lambda bi, i, j: (0, j)),
        ],
        out_specs=pl.BlockSpec((1, tm, tn), lambda bi, i, j: (bi, i, j)),
        out_shape=jax.ShapeDtypeStruct((b, l, n), bf16),
        scratch_shapes=[pltpu.VMEM((tm, d), bf16)],
        compiler_params=_cparams(3),
        name="norm_matmul",
    )(x, g.reshape(1, d), sc, sh, w)


def _matmul_kernel(x_ref, w_ref, o_ref):
    o_ref[...] = jnp.dot(x_ref[...], w_ref[...], preferred_element_type=f32).astype(o_ref.dtype)


def _matmul(x, w, tm, tn):
    m, d = x.shape
    n = w.shape[1]
    tm = min(tm, m)
    return pl.pallas_call(
        _matmul_kernel,
        grid=(m // tm, n // tn),
        in_specs=[pl.BlockSpec((tm, d), lambda i, j: (i, 0)), pl.BlockSpec((d, tn), lambda i, j: (0, j))],
        out_specs=pl.BlockSpec((tm, tn), lambda i, j: (i, j)),
        out_shape=jax.ShapeDtypeStruct((m, n), bf16),
        compiler_params=_cparams(2),
        name="matmul",
    )(x, w)


def _hgrn_kernel(q_ref, ff_ref, fb_ref, v_ref, g_ref, cff_ref, cfb_ref, cv_ref, lb_ref, ng_ref,
                 a_ref, of_ref, ob_ref, qd_ref, kd_ref, ks_ref, dec_ref, st_ref):
    C = SCAN_CHUNK
    G = GATE_ROWS
    L = q_ref.shape[1]
    Lc = cff_ref.shape[1]
    n = L // C
    row = lax.broadcasted_iota(i32, (C, C), 0)
    col = lax.broadcasted_iota(i32, (C, C), 1)
    lower = col <= row
    upper = col >= row
    grow = lax.broadcasted_iota(i32, (G, G), 0)
    gcol = lax.broadcasted_iota(i32, (G, G), 1)
    same_chunk = (grow // C) == (gcol // C)
    tri = ((same_chunk & (gcol <= grow)).astype(bf16), (same_chunk & (gcol >= grow)).astype(bf16))
    lbs = (lb_ref[0:1, :], lb_ref[1:2, :])

    def gate_block(f_raw, d):
        f = lbs[d] + (1.0 - lbs[d]) * jax.nn.sigmoid(f_raw.astype(f32))
        logf = jnp.log(f)
        hi = logf.astype(bf16)
        low = (logf - hi.astype(f32)).astype(bf16)
        cum = jnp.dot(tri[d], hi, preferred_element_type=f32) + jnp.dot(tri[d], low, preferred_element_type=f32)
        k3 = (1.0 - f).reshape(G // C, C, HEAD_DIM)
        cum3 = cum.reshape(G // C, C, HEAD_DIM)
        edge3 = cum3[:, C - 1:C, :] if d == 0 else cum3[:, 0:1, :]
        k_dec = (k3 * jnp.exp(-cum3)).reshape(G, HEAD_DIM).astype(bf16)
        k_st = (k3 * jnp.exp(edge3 - cum3)).reshape(G, HEAD_DIM).astype(bf16)
        dec = [jnp.broadcast_to(jnp.exp(edge3[c]), (HEAD_DIM, HEAD_DIM)).T for c in range(G // C)]
        return cum, k_dec, k_st, dec

    f_refs = (cff_ref, cfb_ref)
    for d in range(2):
        st = jnp.zeros((HEAD_DIM, HEAD_DIM), f32)
        blocks = range(Lc // G) if d == 0 else range(Lc // G - 1, -1, -1)
        for gb in blocks:
            _, _, k_st, dec = gate_block(f_refs[d][0, gb * G:(gb + 1) * G, :], d)
            chunks = range(G // C) if d == 0 else range(G // C - 1, -1, -1)
            for c in chunks:
                v = cv_ref[0, gb * G + c * C:gb * G + (c + 1) * C, :]
                st = (dec[c] * st
                      + lax.dot_general(k_st[c * C:(c + 1) * C, :], v, TN, preferred_element_type=f32))
        st_ref[d] = st

    def precompute(i, carry):
        off = pl.multiple_of(i * G, G)
        rows = pl.ds(off, G)
        qs = _silu(q_ref[0, rows, :].astype(f32))
        for d, f_ref in enumerate((ff_ref, fb_ref)):
            cum, k_dec, k_st, dec = gate_block(f_ref[0, rows, :], d)
            qd_ref[d, rows, :] = (qs * jnp.exp(cum)).astype(bf16)
            kd_ref[d, rows, :] = k_dec
            ks_ref[d, rows, :] = k_st
            for c in range(G // C):
                dec_ref[d, i * (G // C) + c] = dec[c]
        return carry

    lax.fori_loop(0, L // G, precompute, 0, unroll=4)

    def chunk_step(ci, d, mask, o_ref):
        rows = pl.ds(pl.multiple_of(ci * C, C), C)
        qd = qd_ref[d, rows, :]
        v = v_ref[0, rows, :]
        s = lax.dot_general(qd, kd_ref[d, rows, :], NT, preferred_element_type=f32)
        s = jnp.where(mask, s, 0.0).astype(bf16)
        st = st_ref[d]
        o_ref[rows, :] = (jnp.dot(s, v, preferred_element_type=f32)
                          + jnp.dot(qd, st.astype(bf16), preferred_element_type=f32))
        st_ref[d] = dec_ref[d, ci] * st + lax.dot_general(ks_ref[d, rows, :], v, TN, preferred_element_type=f32)

    def body(c, carry):
        chunk_step(c, 0, lower, of_ref)
        chunk_step(n - 1 - c, 1, upper, ob_ref)
        return carry

    lax.fori_loop(0, n, body, 0, unroll=8)

    R = 512 if L % 512 == 0 else C

    def epilogue(r, carry):
        rows = pl.ds(pl.multiple_of(r * R, R), R)
        o = of_ref[rows, :] + ob_ref[rows, :]
        o = o * lax.rsqrt(jnp.mean(o * o, axis=-1, keepdims=True) + EPS)
        o = o * ng_ref[...]
        a_ref[0, rows, :] = (o * _silu(g_ref[0, rows, :].astype(f32))).astype(a_ref.dtype)
        return carry

    lax.fori_loop(0, L // R, epilogue, 0)


def _hgrn(proj, proj_c, lb, norm_g, width_a):
    b, l, _ = proj.shape
    lc = proj_c.shape[1]
    nh = width_a // HEAD_DIM

    def seg(k):
        return pl.BlockSpec((1, l, HEAD_DIM), lambda bi, h, k=k: (bi, 0, k * nh + h))

    def cseg(k):
        return pl.BlockSpec((1, lc, HEAD_DIM), lambda bi, h, k=k: (bi, 0, k * nh + h))

    return pl.pallas_call(
        _hgrn_kernel,
        grid=(b, nh),
        in_specs=[seg(0), seg(1), seg(2), seg(3), seg(4), cseg(0), cseg(1), cseg(2),
                  pl.BlockSpec((2, HEAD_DIM), lambda bi, h: (0, h)),
                  pl.BlockSpec((1, HEAD_DIM), lambda bi, h: (0, h))],
        out_specs=pl.BlockSpec((1, l, HEAD_DIM), lambda bi, h: (bi, 0, h)),
        out_shape=jax.ShapeDtypeStruct((b, l, width_a), bf16),
        scratch_shapes=[pltpu.VMEM((l, HEAD_DIM), f32), pltpu.VMEM((l, HEAD_DIM), f32),
                        pltpu.VMEM((2, l, HEAD_DIM), bf16), pltpu.VMEM((2, l, HEAD_DIM), bf16),
                        pltpu.VMEM((2, l, HEAD_DIM), bf16),
                        pltpu.VMEM((2, l // SCAN_CHUNK, HEAD_DIM, HEAD_DIM), f32),
                        pltpu.VMEM((2, HEAD_DIM, HEAD_DIM), f32)],
        compiler_params=_cparams(2),
        name="hgrn",
    )(proj, proj, proj, proj, proj, proj_c, proj_c, proj_c, lb, norm_g.reshape(1, width_a))


def _residual_norm_route(rows, valid, x, y, gt, g2, sc, sh, wr_ref, br_ref, x1_ref, h2_ref, route_ref, cnt_ref,
                         cnt_acc):
    x1 = x + gt * y
    x1_ref[0, rows, :] = x1
    h2 = _rms_mod(x1, g2, sc, sh)
    h2_ref[0, rows, :] = h2
    logits = jnp.dot(h2.astype(bf16), wr_ref[...], preferred_element_type=f32) + br_ref[...]
    tm = logits.shape[0]
    lane = lax.broadcasted_iota(i32, (tm, ROUTE_LANES), 1)
    lane_f = lane.astype(f32)
    neg = -jnp.inf
    gl = jnp.where(lane < N_EXPERT_GROUPS, logits, neg)
    gmax = jnp.max(gl, axis=-1, keepdims=True)
    p_top = 1.0 / jnp.sum(jnp.exp(gl - gmax), axis=-1, keepdims=True)
    grp = jnp.min(jnp.where(gl == gmax, lane_f, float(ROUTE_LANES)), axis=-1, keepdims=True)
    lo = float(N_EXPERT_GROUPS) + grp * float(EXPERTS_PER_GROUP)
    emask = (lane_f >= lo) & (lane_f < lo + float(EXPERTS_PER_GROUP))
    el = jnp.where(emask, logits, neg)
    emax = jnp.max(el, axis=-1, keepdims=True)
    esum = jnp.sum(jnp.exp(el - emax), axis=-1, keepdims=True)
    i1 = jnp.min(jnp.where(el == emax, lane_f, float(ROUTE_LANES)), axis=-1, keepdims=True)
    p1 = 1.0 / esum
    el2 = jnp.where(lane_f == i1, neg, el)
    emax2 = jnp.max(el2, axis=-1, keepdims=True)
    i2 = jnp.min(jnp.where(el2 == emax2, lane_f, float(ROUTE_LANES)), axis=-1, keepdims=True)
    p2 = jnp.exp(emax2 - emax) / esum
    den = p1 + p2
    w1 = p_top * p1 / den
    w2 = p_top * p2 / den
    e1 = i1 - float(N_EXPERT_GROUPS)
    e2 = i2 - float(N_EXPERT_GROUPS)
    hot1 = lane_f == i1
    hot2 = lane_f == i2
    both = jnp.where((hot1 | hot2) & valid, 1.0, 0.0)
    trow = lax.broadcasted_iota(i32, (tm, tm), 0)
    tcol = lax.broadcasted_iota(i32, (tm, tm), 1)
    before = (tcol < trow).astype(bf16)
    base = jnp.dot(before, both.astype(bf16), preferred_element_type=f32) + cnt_acc[...]
    r1 = jnp.sum(jnp.where(hot1, base, 0.0), axis=-1, keepdims=True)
    r2 = jnp.sum(jnp.where(hot2, base, 0.0), axis=-1, keepdims=True)
    cnt = cnt_acc[...] + jnp.sum(both, axis=0, keepdims=True)
    cnt_acc[...] = cnt
    cnt_ref[...] = cnt
    route = jnp.zeros((tm, ROUTE_LANES), f32)
    for k, val in enumerate((e1, e2, w1, w2, r1, r2)):
        route = jnp.where(lane == k, val, route)
    route_ref[0, rows, :] = route


def _out_pipeline(stage_fn, x_ref, wo_ref, gt_ref, g2_ref, sc_ref, sh_ref, wr_ref, br_ref,
                  x1_ref, h2_ref, route_ref, cnt_ref, t_bufs, y_bufs, cnt_acc):
    p = pl.program_id(0)

    @pl.when(p == 0)
    def _():
        for ref in (*t_bufs, *y_bufs, cnt_acc):
            ref[...] = jnp.zeros_like(ref)

    for half in range(2):
        rows = slice(half * SUB, (half + 1) * SUB)
        _residual_norm_route(rows, p > 0, x_ref[0, rows, :], y_bufs[half][...], gt_ref[0], g2_ref[...], sc_ref[0],
                             sh_ref[0], wr_ref, br_ref, x1_ref, h2_ref, route_ref, cnt_ref, cnt_acc)
        y_bufs[1 - half][...] = jnp.dot(t_bufs[1 - half][...], wo_ref[...], preferred_element_type=f32)
        stage_fn(rows, t_bufs[half])


def _gelu_tanh(v):
    return 0.5 * v * (1.0 + jnp.tanh(0.7978845608028654 * (v + 0.044715 * (v * v * v))))


def _even_out_kernel(a_ref, u_ref, v_ref, lng_ref, lnb_ref, ws_ref, bs_ref, x_ref, wo_ref, gt_ref, g2_ref, sc_ref,
                     sh_ref, wr_ref, br_ref, x1_ref, h2_ref, route_ref, cnt_ref, t0, t1, y0, y1, cnt_acc):
    wa = a_ref.shape[2]
    gd = u_ref.shape[2] // N_GROUPS_B

    def stage(rows, t_ref):
        t_ref[:, 0:wa] = a_ref[0, rows, :]
        gv = _gelu_tanh(v_ref[0, rows, :].astype(f32))
        mu = jnp.mean(gv, axis=-1, keepdims=True)
        dv = gv - mu
        var = jnp.mean(dv * dv, axis=-1, keepdims=True)
        vn = (dv * lax.rsqrt(var + EPS) * lng_ref[...] + lnb_ref[...]).astype(bf16)
        for nb in range(SUB // CHUNK_B):
            rs = slice(nb * CHUNK_B, (nb + 1) * CHUNK_B)
            ru = slice(rows.start + nb * CHUNK_B, rows.start + (nb + 1) * CHUNK_B)
            for g in range(N_GROUPS_B):
                cs = slice(g * gd, (g + 1) * gd)
                s = jnp.dot(ws_ref[g], vn[rs, cs], preferred_element_type=f32) + bs_ref[:, g:g + 1]
                t_ref[rs, wa + g * gd:wa + (g + 1) * gd] = (_gelu_tanh(u_ref[0, ru, cs].astype(f32)) * s).astype(bf16)

    _out_pipeline(stage, x_ref, wo_ref, gt_ref, g2_ref, sc_ref, sh_ref, wr_ref, br_ref,
                  x1_ref, h2_ref, route_ref, cnt_ref, (t0, t1), (y0, y1), cnt_acc)


def _odd_out_kernel(hx_ref, bg_ref, cg_ref, cw_ref, x_ref, wo_ref, gt_ref, g2_ref, sc_ref,
                    sh_ref, wr_ref, br_ref, x1_ref, h2_ref, route_ref, cnt_ref, t0, t1, y0, y1, cnt_acc):
    def stage(rows, t_ref):
        z = cg_ref[0, rows, :].astype(f32) * hx_ref[0, rows, :].astype(f32)
        pos = lax.broadcasted_iota(i32, (SUB, 1), 0) % GRID_W
        z_prev = jnp.where(pos == 0, 0.0, pltpu.roll(z, 1, 0))
        z_next = jnp.where(pos == GRID_W - 1, 0.0, pltpu.roll(z, SUB - 1, 0))
        conv = z_prev * cw_ref[0:1, :] + z * cw_ref[1:2, :] + z_next * cw_ref[2:3, :]
        t_ref[...] = (bg_ref[0, rows, :].astype(f32) * conv).astype(bf16)

    _out_pipeline(stage, x_ref, wo_ref, gt_ref, g2_ref, sc_ref, sh_ref, wr_ref, br_ref,
                  x1_ref, h2_ref, route_ref, cnt_ref, (t0, t1), (y0, y1), cnt_acc)


def _mixer_out(kernel_fn, name, stage_args, stage_specs, x, w_out, gt, g2, sc, sh, wr, br):
    b, l, d = x.shape
    blk = 2 * SUB
    per_b = l // blk
    n = b * per_b
    k = w_out.shape[0]

    def ahead(p):
        q = jnp.minimum(p, n - 1)
        return q // per_b, q % per_b

    def behind(p):
        q = jnp.maximum(p - 1, 0)
        return q // per_b, q % per_b

    const = lambda shape: pl.BlockSpec(shape, lambda p: (0,) * len(shape))
    per_b_vec = pl.BlockSpec((1, 1, d), lambda p: (behind(p)[0], 0, 0))
    tile = lambda width: pl.BlockSpec((1, blk, width), lambda p: (*behind(p), 0))
    return pl.pallas_call(
        kernel_fn,
        grid=(n + 1,),
        in_specs=stage_specs(ahead, blk, const) + [
            tile(d), const((k, d)), per_b_vec, const((1, d)), per_b_vec, per_b_vec,
            const((d, ROUTE_LANES)), const((1, ROUTE_LANES))],
        out_specs=[tile(d), tile(d), tile(ROUTE_LANES), const((1, ROUTE_LANES))],
        out_shape=[jax.ShapeDtypeStruct((b, l, d), f32), jax.ShapeDtypeStruct((b, l, d), f32),
                   jax.ShapeDtypeStruct((b, l, ROUTE_LANES), f32), jax.ShapeDtypeStruct((1, ROUTE_LANES), f32)],
        scratch_shapes=[pltpu.VMEM((SUB, k), bf16), pltpu.VMEM((SUB, k), bf16),
                        pltpu.VMEM((SUB, d), f32), pltpu.VMEM((SUB, d), f32), pltpu.VMEM((1, ROUTE_LANES), f32)],
        compiler_params=_cparams(1),
        name=name,
    )(*stage_args, x, w_out, gt, g2.reshape(1, d), sc, sh, wr, br)


def _even_out(x, a, proj, ln_g, ln_b, w_s, b_s_t, w_out, gt, g2, sc, sh, wr, br):
    wa = a.shape[2]
    wb = ln_g.shape[0]
    ub = (5 * wa) // wb

    def stage_specs(ahead, blk, const):
        return [pl.BlockSpec((1, blk, wa), lambda p: (*ahead(p), 0)),
                pl.BlockSpec((1, blk, wb), lambda p: (*ahead(p), ub)),
                pl.BlockSpec((1, blk, wb), lambda p: (*ahead(p), ub + 1)),
                const((1, wb)), const((1, wb)), const((N_GROUPS_B, CHUNK_B, CHUNK_B)), const((CHUNK_B, N_GROUPS_B))]

    return _mixer_out(_even_out_kernel, "even_out",
                      (a, proj, proj, ln_g.reshape(1, wb), ln_b.reshape(1, wb), w_s, b_s_t), stage_specs,
                      x, w_out, gt, g2, sc, sh, wr, br)


def _odd_out(x, pc, conv_w, w_out, gt, g2, sc, sh, wr, br):
    wc = w_out.shape[0]

    def stage_specs(ahead, blk, const):
        return [pl.BlockSpec((1, blk, wc), lambda p, j=j: (*ahead(p), j)) for j in range(3)] + [const(conv_w.shape)]

    return _mixer_out(_odd_out_kernel, "odd_out", (pc, pc, pc, conv_w), stage_specs,
                      x, w_out, gt, g2, sc, sh, wr, br)


def _row_gather_start(idx_ref, n_rows, src_hbm, dst_buf, sem):
    def body(r, carry):
        t = idx_ref[0, 0, r]
        pltpu.make_async_copy(src_hbm.at[pl.ds(t, 1), :], dst_buf.at[pl.ds(r, 1), :], sem).start()
        return carry

    lax.fori_loop(0, n_rows, body, 0, unroll=8)


def _row_gather_wait(n_rows, src_hbm, dst_buf, sem):
    def body(r, carry):
        pltpu.make_async_copy(src_hbm.at[pl.ds(0, 1), :], dst_buf.at[pl.ds(r, 1), :], sem).wait()
        return carry

    lax.fori_loop(0, n_rows, body, 0, unroll=8)


def _dispatch_kernel(pos_ref, h_ref, xs_hbm, sem):
    tm = h_ref.shape[0]

    def start(r, carry):
        for k in range(TOP_K):
            pltpu.make_async_copy(h_ref.at[pl.ds(r, 1), :], xs_hbm.at[pl.ds(pos_ref[0, 0, k * tm + r], 1), :],
                                  sem.at[0]).start()
        return carry

    lax.fori_loop(0, tm, start, 0, unroll=8)

    def wait(r, carry):
        pltpu.make_async_copy(h_ref.at[pl.ds(0, 1), :], xs_hbm.at[pl.ds(0, 1), :], sem.at[0]).wait()
        return carry

    lax.fori_loop(0, TOP_K * tm, wait, 0, unroll=8)


def _pos_blocks(pos, tm):
    nt = pos.shape[0] // tm
    return pos.reshape(nt, tm, TOP_K).transpose(0, 2, 1).reshape(nt, 1, TOP_K * tm)


def _moe_dispatch(h2, pos, tm):
    t, d = h2.shape
    return pl.pallas_call(
        _dispatch_kernel,
        grid=(t // tm,),
        in_specs=[pl.BlockSpec((1, 1, TOP_K * tm), lambda i: (i, 0, 0), memory_space=pltpu.SMEM),
                  pl.BlockSpec((tm, d), lambda i: (i, 0))],
        out_specs=pl.BlockSpec(memory_space=pl.ANY),
        out_shape=jax.ShapeDtypeStruct((t * TOP_K, d), h2.dtype),
        scratch_shapes=[pltpu.SemaphoreType.DMA((1,))],
        compiler_params=_cparams(1),
        name="moe_dispatch",
    )(_pos_blocks(pos, tm), h2)


def _ffn_kernel(blk_ref, e_ref, lo_ref, hi_ref, first_ref, nxt_ref, x_ref, wg_hbm, wu_hbm, wd_hbm, y_ref,
                wg_f, wu_f, wd_f, wg_s, wu_s, wd_s, sem, *, layer):
    w = pl.program_id(0)
    lo = lo_ref[w]
    hi = hi_ref[w]
    prev = jnp.maximum(w - 1, 0)

    def weight_copies(e):
        return (pltpu.make_async_copy(wg_hbm.at[layer, e], wg_f, sem.at[0]),
                pltpu.make_async_copy(wu_hbm.at[layer, e], wu_f, sem.at[1]),
                pltpu.make_async_copy(wd_hbm.at[layer, e], wd_f, sem.at[2]))

    @pl.when(hi > lo)
    def _():
        @pl.when(first_ref[w] == 1)
        def _():
            @pl.when(w == 0)
            def _():
                for cp in weight_copies(e_ref[0]):
                    cp.start()

            for cp in weight_copies(e_ref[w]):
                cp.wait()
            wg_s[...] = wg_f[...].astype(bf16)
            wu_s[...] = wu_f[...].astype(bf16)
            wd_s[...] = wd_f[...].astype(bf16)

            @pl.when(nxt_ref[w] >= 0)
            def _():
                for cp in weight_copies(nxt_ref[w]):
                    cp.start()

        xb = x_ref[...].astype(bf16)
        gate = jnp.dot(xb, wg_s[...], preferred_element_type=f32)
        up = jnp.dot(xb, wu_s[...], preferred_element_type=f32)
        mid = (_silu(gate) * up).astype(bf16)
        y = jnp.dot(mid, wd_s[...], preferred_element_type=f32)
        r = lax.broadcasted_iota(i32, (y.shape[0], 1), 0)
        mine = (r >= lo) & (r < hi)
        first_visit = (w == 0) | (blk_ref[w] != blk_ref[prev])

        @pl.when(first_visit)
        def _():
            y_ref[...] = jnp.where(mine, y, 0.0)

        @pl.when(jnp.logical_not(first_visit))
        def _():
            y_ref[...] = jnp.where(mine, y, y_ref[...])


def _moe_ffn(xs, items, layer, w_gate, w_up, w_down):
    rows, d = xs.shape
    bm = MOE_BLOCK
    ff = w_gate.shape[3]
    n_items = items[0].shape[0]
    grid_spec = pltpu.PrefetchScalarGridSpec(
        num_scalar_prefetch=6,
        grid=(n_items,),
        in_specs=[
            pl.BlockSpec((bm, d), lambda w, blk, *_: (blk[w], 0)),
            pl.BlockSpec(memory_space=pl.ANY),
            pl.BlockSpec(memory_space=pl.ANY),
            pl.BlockSpec(memory_space=pl.ANY),
        ],
        out_specs=pl.BlockSpec((bm, d), lambda w, blk, *_: (blk[w], 0)),
        scratch_shapes=[pltpu.VMEM((d, ff), f32), pltpu.VMEM((d, ff), f32), pltpu.VMEM((ff, d), f32),
                        pltpu.VMEM((d, ff), bf16), pltpu.VMEM((d, ff), bf16), pltpu.VMEM((ff, d), bf16),
                        pltpu.SemaphoreType.DMA((3,))],
    )
    return pl.pallas_call(
        functools.partial(_ffn_kernel, layer=layer),
        grid_spec=grid_spec,
        out_shape=jax.ShapeDtypeStruct((rows, d), f32),
        compiler_params=_cparams(1),
        name="moe_ffn",
    )(*items, xs, w_gate, w_up, w_down)


def _combine_kernel(pos_cur_ref, pos_nxt_ref, x_ref, gt_ref, ng_ref, nsc_ref, nsh_ref, route_ref, ys_hbm, *rest, final):
    out_refs, (ybuf, sem) = rest[:-2], rest[-2:]
    i = pl.program_id(0)
    n = pl.num_programs(0)
    tm = x_ref.shape[1]
    slot = lax.rem(i, 2)

    @pl.when(i == 0)
    def _():
        _row_gather_start(pos_cur_ref, TOP_K * tm, ys_hbm, ybuf.at[0], sem.at[0])

    @pl.when(i + 1 < n)
    def _():
        _row_gather_start(pos_nxt_ref, TOP_K * tm, ys_hbm, ybuf.at[1 - slot], sem.at[1 - slot])

    _row_gather_wait(TOP_K * tm, ys_hbm, ybuf.at[slot], sem.at[slot])
    route = route_ref[0]
    f = ybuf[slot, 0:tm, :] * route[:, TOP_K:TOP_K + 1] + ybuf[slot, tm:2 * tm, :] * route[:, TOP_K + 1:TOP_K + 2]
    xo = x_ref[0] + gt_ref[0] * f
    hn = _rms_mod(xo, ng_ref[...], nsc_ref[0], nsh_ref[0])
    if final:
        out_refs[0][0] = hn
    else:
        out_refs[0][0] = xo
        out_refs[1][0] = hn.astype(out_refs[1].dtype)


def _moe_combine(x1, gt, norm_g, norm_sc, norm_sh, route, ys, pos, tm, final):
    b, l, d = x1.shape
    nt = (b * l) // tm
    per_b = l // tm
    pos3 = _pos_blocks(pos, tm)
    tile = pl.BlockSpec((1, tm, d), lambda i: (i // per_b, i % per_b, 0))
    per_b_vec = pl.BlockSpec((1, 1, d), lambda i: (i // per_b, 0, 0))
    if final:
        out_specs, out_shape = tile, jax.ShapeDtypeStruct((b, l, d), f32)
    else:
        out_specs = [tile, tile]
        out_shape = [jax.ShapeDtypeStruct((b, l, d), f32), jax.ShapeDtypeStruct((b, l, d), bf16)]
    return pl.pallas_call(
        functools.partial(_combine_kernel, final=final),
        grid=(nt,),
        in_specs=[
            pl.BlockSpec((1, 1, TOP_K * tm), lambda i: (i, 0, 0), memory_space=pltpu.SMEM),
            pl.BlockSpec((1, 1, TOP_K * tm), lambda i: (jnp.minimum(i + 1, nt - 1), 0, 0), memory_space=pltpu.SMEM),
            tile, per_b_vec,
            pl.BlockSpec((1, d), lambda i: (0, 0)),
            per_b_vec, per_b_vec,
            pl.BlockSpec((1, tm, ROUTE_LANES), lambda i: (i // per_b, i % per_b, 0)),
            pl.BlockSpec(memory_space=pl.ANY),
        ],
        out_specs=out_specs,
        out_shape=out_shape,
        scratch_shapes=[pltpu.VMEM((2, TOP_K * tm, d), f32), pltpu.SemaphoreType.DMA((2,))],
        compiler_params=_cparams(1),
        name="moe_combine",
    )(pos3, pos3, x1, gt, norm_g.reshape(1, d), norm_sc, norm_sh, route, ys)


def _dispatch_plan(route, cnt, n_tok):
    bm = MOE_BLOCK
    n_rows = n_tok * TOP_K
    n_blk = n_rows // bm
    experts = jnp.arange(N_EXPERTS, dtype=i32)
    counts = cnt[0, N_EXPERT_GROUPS:N_EXPERT_GROUPS + N_EXPERTS].astype(i32)
    ends = jnp.cumsum(counts)
    starts = ends - counts
    eid = route[..., 0:TOP_K].astype(i32).reshape(n_tok, TOP_K)
    rank = route[..., 2 * TOP_K:3 * TOP_K].astype(i32).reshape(n_tok, TOP_K)
    pos = rank + jnp.sum(jnp.where(eid[..., None] == experts, starts, 0), axis=-1)
    first_blk = starts // bm
    n_it = jnp.where(counts > 0, (ends - 1) // bm - first_blk + 1, 0)
    it_end = jnp.cumsum(n_it)
    it_start = it_end - n_it
    w = jnp.arange(n_blk + N_EXPERTS, dtype=i32)
    live = w < it_end[-1]
    e_last = jnp.max(jnp.where(counts > 0, experts, 0))
    e_w = jnp.where(live, jnp.sum((it_end[None, :] <= w[:, None]).astype(i32), axis=1), e_last)
    hot = e_w[:, None] == experts
    pick = lambda tab: jnp.sum(jnp.where(hot, tab, 0), axis=1)
    blk_w = jnp.where(live, pick(first_blk) + w - pick(it_start), n_blk - 1)
    lo_w = jnp.where(live, jnp.maximum(pick(starts), blk_w * bm) - blk_w * bm, 0)
    hi_w = jnp.where(live, jnp.minimum(pick(ends), (blk_w + 1) * bm) - blk_w * bm, 0)
    first_w = (live & ((w == 0) | (e_w != jnp.roll(e_w, 1)))).astype(i32)
    later = (experts[None, :] > experts[:, None]) & (counts[None, :] > 0)
    nxt_tab = jnp.min(jnp.where(later, experts[None, :], N_EXPERTS), axis=1)
    nxt_w = jnp.where(live, pick(jnp.where(nxt_tab < N_EXPERTS, nxt_tab, -1)), -1)
    return pos, (blk_w, e_w, lo_w, hi_w, first_w, nxt_w)


def _hier_moe(x1, h2, route, cnt, gt, norm_g, norm_sc, norm_sh, layer, w_gate, w_up, w_down, final):
    b, l, d = x1.shape
    n_tok = b * l
    assert (n_tok * TOP_K) % MOE_BLOCK == 0
    pos, items = _dispatch_plan(route, cnt, n_tok)
    xs = _moe_dispatch(h2.reshape(n_tok, d), pos, min(n_tok, 1024))
    ys = _moe_ffn(xs, items, layer, w_gate, w_up, w_down)
    return _moe_combine(x1, gt, norm_g, norm_sc, norm_sh, route, ys, pos, 512, final)


def _router_params(rg_w, rg_b, re_w, re_b):
    d = rg_w.shape[0]
    pad = ROUTE_LANES - N_EXPERT_GROUPS - N_EXPERTS
    wr = jnp.concatenate([rg_w, re_w, jnp.zeros((d, pad), f32)], axis=1).astype(bf16)
    br = jnp.concatenate([rg_b, re_b, jnp.zeros((pad,), f32)]).reshape(1, ROUTE_LANES)
    return wr, br


def kernel(x, c, ctx, c_ctx, w_mod, b_mod, norm_mix_g, norm_ffn_g, final_g, even_w_in, even_w_out, hgrn_lb_raw,
           hgrn_norm_g, gmlp_ln_g, gmlp_ln_b, gmlp_w_s, gmlp_b_s, odd_w_in, odd_conv_w, odd_w_out, router_g_w,
           router_g_b, router_e_w, router_e_b, exp_w_gate, exp_w_up, exp_w_down):
    b, l, d = x.shape
    depth = w_mod.shape[0]
    assert depth == 2, "layer plan below is written for one even and one odd layer"
    width_a = hgrn_norm_g.shape[1]
    lc = ctx.shape[1]

    rows = -(-(b + 1) // 8) * 8
    c_all = jnp.zeros((rows, d), f32).at[:b].set(c).at[b].set(c_ctx)
    mod = _mod(c_all, w_mod, b_mod)

    def latent_mod(layer):
        return [m.reshape(b, 1, d) for m in jnp.split(mod[layer, :b], 6, axis=-1)]

    sh_m, sc_m, gt_m, sh_f, sc_f, gt_f = latent_mod(0)
    csh_m = jnp.broadcast_to(mod[0, b, 0:d].reshape(1, 1, d), (b, 1, d))
    csc_m = jnp.broadcast_to(mod[0, b, d:2 * d].reshape(1, 1, d), (b, 1, d))
    w_in = even_w_in[0].astype(bf16)
    proj = _norm_matmul(x, norm_mix_g[0], sc_m, sh_m, w_in, min(l, 1024), 1792)
    proj_c = _norm_matmul(ctx, norm_mix_g[0], csc_m, csh_m, w_in[:, width_a:4 * width_a], lc, 1024)
    lb = jnp.cumsum(jax.nn.softmax(hgrn_lb_raw.astype(f32), axis=0), axis=0)[0]
    a = _hgrn(proj, proj_c, lb, hgrn_norm_g[0], width_a)
    wr, br = _router_params(router_g_w[0], router_g_b[0], router_e_w[0], router_e_b[0])
    x1, h2, route, cnt = _even_out(x, a, proj, gmlp_ln_g[0], gmlp_ln_b[0], gmlp_w_s[0].astype(bf16), gmlp_b_s[0].T,
                                   even_w_out[0].astype(bf16), gt_m, norm_ffn_g[0], sc_f, sh_f, wr, br)
    sh_m1, sc_m1, gt_m1, sh_f1, sc_f1, gt_f1 = latent_mod(1)
    x, h = _hier_moe(x1, h2, route, cnt, gt_f, norm_mix_g[1], sc_m1, sh_m1, 0, exp_w_gate, exp_w_up, exp_w_down, False)

    pc = _matmul(h.reshape(b * l, d), odd_w_in[0].astype(bf16), 2048, 1536).reshape(b, l, -1)
    wr, br = _router_params(router_g_w[1], router_g_b[1], router_e_w[1], router_e_b[1])
    x1, h2, route, cnt = _odd_out(x, pc, odd_conv_w[0], odd_w_out[0].astype(bf16), gt_m1, norm_ffn_g[1], sc_f1, sh_f1,
                                  wr, br)
    zero = jnp.zeros((b, 1, d), f32)
    return _hier_moe(x1, h2, route, cnt, gt_f1, final_g, zero, zero, 1, exp_w_gate, exp_w_up, exp_w_down, True)
```
